```python
import math
import jax, jax.numpy as jnp
from jax import lax
import numpy as np

D_MODEL = 1024
BATCH = 8
SEQ = 4096
DEPTH = 4

HEAD_DIM = 64
ATT_GROUPS = ((128, 1), (512, 4), (2048, 16))
N_GROUPS = len(ATT_GROUPS)
HEADS_PER_GROUP = 4
N_ATT_QKV_HEADS = N_GROUPS * HEADS_PER_GROUP
D_ATT_QKV = N_ATT_QKV_HEADS * HEAD_DIM
D_ATT_OUT = HEADS_PER_GROUP * HEAD_DIM
ALIBI_MAX_BIAS = 8.0
D_LRU = (3 * D_MODEL) // 4
LRU_BLOCK = 64
N_LRU_BLOCKS = D_LRU // LRU_BLOCK
CONV_WIDTH = 4
LRU_C = 8.0
D_MIX = D_ATT_OUT + D_LRU
D_IN_PROJ = 3 * D_ATT_QKV + 2 * D_LRU
D_FF = 2816
FFN_RES_WEIGHT = 0.5
RMS_EPS = 1e-6
NEG_INF = -1e30

kernel_name = "hybrid_dilated_attn_rglru_macaron"


def rms_norm(x, gain):
    xf = x.astype(jnp.float32)
    y = xf * lax.rsqrt(jnp.mean(xf * xf, axis=-1, keepdims=True) + RMS_EPS)
    return (y * gain.astype(jnp.float32)).astype(x.dtype)


def swiglu(h, w_gate, w_up, w_down):
    return (jax.nn.silu(h @ w_gate) * (h @ w_up)) @ w_down


def alibi_slopes():
    h = jnp.arange(1, N_ATT_QKV_HEADS + 1, dtype=jnp.float32)
    return jnp.exp2(-ALIBI_MAX_BIAS * h / N_ATT_QKV_HEADS).reshape(N_GROUPS, HEADS_PER_GROUP)


def dilated_window_attention(q, k, v, window, dilation, slopes):
    B, S, H, Dh = q.shape
    n = window // dilation
    span = n * dilation
    s_pad = -(-S // span) * span
    L = s_pad // dilation
    nb = L // n

    def to_blocks(t):
        t = jnp.pad(t.astype(jnp.float32), ((0, 0), (0, s_pad - S), (0, 0), (0, 0)))
        t = t.reshape(B, L, dilation, H, Dh).transpose(0, 2, 3, 1, 4)
        return t.reshape(B, dilation, H, nb, n, Dh)

    def with_prev(t):
        prev = jnp.concatenate([jnp.zeros_like(t[:, :, :, :1]), t[:, :, :, :-1]], axis=3)
        return jnp.concatenate([prev, t], axis=4)

    qb = to_blocks(q)
    kw = with_prev(to_blocks(k))
    vw = with_prev(to_blocks(v))
    scores = jnp.einsum('brhnid,brhnkd->brhnik', qb, kw) / math.sqrt(Dh)
    qi = jnp.arange(n)[:, None]
    ki = jnp.arange(2 * n)[None, :]
    steps = n + qi - ki
    blk = jnp.arange(nb)[:, None, None]
    valid = (steps >= 0) & (steps <= n) & (blk * n + ki - n >= 0)
    bias = -slopes[:, None, None, None] * (dilation * steps).astype(jnp.float32)
    scores = jnp.where(valid, scores + bias, NEG_INF)
    lse = jax.nn.logsumexp(scores, axis=-1)
    p = jnp.exp(scores - lse[..., None])
    out = jnp.einsum('brhnik,brhnkd->brhnid', p, vw)
    out = out.reshape(B, dilation, H, L, Dh).transpose(0, 3, 1, 2, 4).reshape(B, s_pad, H, Dh)[:, :S]
    lse = lse.reshape(B, dilation, H, L).transpose(0, 3, 1, 2).reshape(B, s_pad, H)[:, :S]
    return out, lse


def causal_depthwise_conv(x, w, b):
    y = lax.conv_general_dilated(
        x, w[:, None, :].astype(x.dtype), window_strides=(1,),
        padding=((CONV_WIDTH - 1, 0),),
        dimension_numbers=('NWC', 'WIO', 'NWC'),
        feature_group_count=x.shape[-1])
    return y + b


def _linear_recurrence_combine(c1, c2):
    a1, b1 = c1
    a2, b2 = c2
    return a1 * a2, a2 * b1 + b2


def rg_lru(x, gate_a_w, gate_a_b, gate_x_w, gate_x_b, lam):
    B, S, C = x.shape
    xf = x.astype(jnp.float32)
    xh = xf.reshape(B, S, N_LRU_BLOCKS, LRU_BLOCK)
    r = jax.nn.sigmoid(jnp.einsum('bshi,hij->bshj', xh, gate_a_w.astype(jnp.float32)).reshape(B, S, C)
                       + gate_a_b.astype(jnp.float32))
    ig = jax.nn.sigmoid(jnp.einsum('bshi,hij->bshj', xh, gate_x_w.astype(jnp.float32)).reshape(B, S, C)
                        + gate_x_b.astype(jnp.float32))
    log_a = -LRU_C * r * jax.nn.softplus(-lam.astype(jnp.float32))
    a = jnp.exp(log_a)
    b = jnp.sqrt(-jnp.expm1(2.0 * log_a)) * (ig * xf)
    _, h = lax.associative_scan(_linear_recurrence_combine, (a, b), axis=1)
    return h.astype(x.dtype)


def setup_inputs(seed: int = 0) -> dict:
    key = jax.random.key(seed)
    ks = jax.random.split(key, 24)

    def normal(k, shape, scale):
        return jax.random.normal(k, shape, jnp.float32) * scale

    def gain(k, shape):
        return 1.0 + 0.02 * jax.random.normal(k, shape, jnp.float32)

    u = jax.random.uniform(ks[14], (DEPTH, D_LRU), jnp.float32, 0.9, 0.999)
    a0 = u ** (1.0 / LRU_C)
    lru_lambda = jnp.log(a0) - jnp.log1p(-a0)
    return {
        "x": normal(ks[0], (BATCH, SEQ, D_MODEL), 1.0),
        "ffn1_norm": gain(ks[1], (DEPTH, D_MODEL)),
        "ffn1_w_gate": normal(ks[2], (DEPTH, D_MODEL, D_FF), D_MODEL ** -0.5),
        "ffn1_w_up": normal(ks[3], (DEPTH, D_MODEL, D_FF), D_MODEL ** -0.5),
        "ffn1_w_down": normal(ks[4], (DEPTH, D_FF, D_MODEL), D_FF ** -0.5),
        "mix_norm": gain(ks[5], (DEPTH, D_MODEL)),
        "w_in": normal(ks[6], (DEPTH, D_MODEL, D_IN_PROJ), D_MODEL ** -0.5),
        "q_norm": gain(ks[7], (DEPTH, HEAD_DIM)),
        "k_norm": gain(ks[8], (DEPTH, HEAD_DIM)),
        "conv_w": normal(ks[9], (DEPTH, CONV_WIDTH, D_LRU), CONV_WIDTH ** -0.5),
        "conv_b": normal(ks[10], (DEPTH, D_LRU), 0.01),
        "gate_a_w": normal(ks[11], (DEPTH, N_LRU_BLOCKS, LRU_BLOCK, LRU_BLOCK), LRU_BLOCK ** -0.5),
        "gate_a_b": normal(ks[12], (DEPTH, D_LRU), 0.01),
        "gate_x_w": normal(ks[13], (DEPTH, N_LRU_BLOCKS, LRU_BLOCK, LRU_BLOCK), LRU_BLOCK ** -0.5),
        "gate_x_b": normal(ks[15], (DEPTH, D_LRU), 0.01),
        "lru_lambda": lru_lambda,
        "w_out": normal(ks[16], (DEPTH, D_MIX, D_MODEL), D_MIX ** -0.5),
        "ffn2_norm": gain(ks[17], (DEPTH, D_MODEL)),
        "ffn2_w_gate": normal(ks[18], (DEPTH, D_MODEL, D_FF), D_MODEL ** -0.5),
        "ffn2_w_up": normal(ks[19], (DEPTH, D_MODEL, D_FF), D_MODEL ** -0.5),
        "ffn2_w_down": normal(ks[20], (DEPTH, D_FF, D_MODEL), D_FF ** -0.5),
    }


def reference(x, ffn1_norm, ffn1_w_gate, ffn1_w_up, ffn1_w_down, mix_norm, w_in,
              q_norm, k_norm, conv_w, conv_b, gate_a_w, gate_a_b, gate_x_w, gate_x_b,
              lru_lambda, w_out, ffn2_norm, ffn2_w_gate, ffn2_w_up, ffn2_w_down):
    B, S, _ = x.shape
    slopes = alibi_slopes()
    for l in range(DEPTH):
        x = x + FFN_RES_WEIGHT * swiglu(rms_norm(x, ffn1_norm[l]), ffn1_w_gate[l], ffn1_w_up[l], ffn1_w_down[l])

        h = rms_norm(x, mix_norm[l])
        proj = h @ w_in[l]
        q, k, v, lru_x, lru_gate = jnp.split(
            proj, [D_ATT_QKV, 2 * D_ATT_QKV, 3 * D_ATT_QKV, 3 * D_ATT_QKV + D_LRU], axis=-1)
        q = rms_norm(q.reshape(B, S, N_GROUPS, HEADS_PER_GROUP, HEAD_DIM), q_norm[l])
        k = rms_norm(k.reshape(B, S, N_GROUPS, HEADS_PER_GROUP, HEAD_DIM), k_norm[l])
        v = v.reshape(B, S, N_GROUPS, HEADS_PER_GROUP, HEAD_DIM)

        outs, lses = [], []
        for g, (window, dilation) in enumerate(ATT_GROUPS):
            o_g, lse_g = dilated_window_attention(q[:, :, g], k[:, :, g], v[:, :, g], window, dilation, slopes[g])
            outs.append(o_g)
            lses.append(lse_g)
        w_grp = jax.nn.softmax(jnp.stack(lses, axis=0), axis=0)
        att = jnp.sum(w_grp[..., None] * jnp.stack(outs, axis=0), axis=0)
        att = att.reshape(B, S, D_ATT_OUT).astype(x.dtype)

        xc = causal_depthwise_conv(lru_x, conv_w[l], conv_b[l])
        y = rg_lru(xc, gate_a_w[l], gate_a_b[l], gate_x_w[l], gate_x_b[l], lru_lambda[l])
        y = y * jax.nn.gelu(lru_gate)

        x = x + jnp.concatenate([att, y], axis=-1) @ w_out[l]

        x = x + FFN_RES_WEIGHT * swiglu(rms_norm(x, ffn2_norm[l]), ffn2_w_gate[l], ffn2_w_up[l], ffn2_w_down[l])
    return x
```

```python
import functools
import math

import jax
import jax.numpy as jnp
from jax import lax
from jax.experimental import pallas as pl
from jax.experimental.pallas import tpu as pltpu

F32 = jnp.float32
BF16 = jnp.bfloat16

D_MODEL = 1024
HEAD_DIM = 64
ATT_GROUPS = ((128, 1), (512, 4), (2048, 16))
N_GROUPS = len(ATT_GROUPS)
HEADS_PER_GROUP = 4
N_QKV_HEADS = N_GROUPS * HEADS_PER_GROUP
D_ATT_QKV = N_QKV_HEADS * HEAD_DIM
D_ATT_OUT = HEADS_PER_GROUP * HEAD_DIM
ALIBI_MAX_BIAS = 8.0
D_LRU = 768
LRU_BLOCK = 64
CONV_WIDTH = 4
LRU_C = 8.0
D_FF = 2816
FFN_RES_WEIGHT = 0.5
RMS_EPS = 1e-6
NEG_INF = -1e30

LANES = 128
SUBLANES = 8
MXU_DIM = 256
VMEM_LIMIT = 56 * 1024 * 1024

FF_CHUNK = MXU_DIM
N_FF_CHUNKS = D_FF // FF_CHUNK
TM_FFN = 512
TM_PROJ = 512
ATT_BLOCK = 128
HEAD_PAIR = LANES
N_PAIRS = D_ATT_OUT // HEAD_PAIR
MAX_DIL = ATT_GROUPS[-1][1]
LRU_TS = 128
LRU_PITCH = LRU_TS + SUBLANES
LRU_GATE_BLK = MXU_DIM
N_LRU_LANE_TILES = D_LRU // LANES


def _resident(block_shape, index_map):
    return pl.BlockSpec(block_shape, index_map, pipeline_mode=pl.Buffered(1))


def _rms_norm(x, gain):
    ms = jnp.mean(x * x, axis=-1, keepdims=True)
    return x * lax.rsqrt(ms + RMS_EPS) * gain


def _ffn_kernel(x_ref, gain_ref, wgu_ref, wd_ref, o_ref, acc_ref):
    x = x_ref[...]
    h = _rms_norm(x, gain_ref[...]).astype(BF16)
    for j in range(N_FF_CHUNKS):
        gu = jnp.dot(h, wgu_ref[j], preferred_element_type=F32)
        g = gu[:, :FF_CHUNK]
        u = gu[:, FF_CHUNK:]
        a = (g / (1.0 + jnp.exp(-g)) * u).astype(BF16)
        d = jnp.dot(a, wd_ref[j], preferred_element_type=F32)
        if j == 0:
            acc_ref[...] = d
        else:
            acc_ref[...] += d
    o_ref[...] = x + FFN_RES_WEIGHT * acc_ref[...]


def _ffn(x, gain, wgu, wd):
    n = x.shape[0]
    return pl.pallas_call(
        _ffn_kernel,
        out_shape=jax.ShapeDtypeStruct((n, D_MODEL), F32),
        grid=(n // TM_FFN,),
        in_specs=[
            pl.BlockSpec((TM_FFN, D_MODEL), lambda i: (i, 0)),
            _resident((1, D_MODEL), lambda i: (0, 0)),
            _resident((N_FF_CHUNKS, D_MODEL, 2 * FF_CHUNK), lambda i: (0, 0, 0)),
            _resident((N_FF_CHUNKS, FF_CHUNK, D_MODEL), lambda i: (0, 0, 0)),
        ],
        out_specs=pl.BlockSpec((TM_FFN, D_MODEL), lambda i: (i, 0)),
        scratch_shapes=[pltpu.VMEM((TM_FFN, D_MODEL), F32)],
        compiler_params=pltpu.CompilerParams(
            dimension_semantics=("arbitrary",), vmem_limit_bytes=VMEM_LIMIT),
        name="ffn",
    )(x, gain, wgu, wd)


def _inproj_kernel(x_ref, gain_ref, w_ref, hm_ref, qg_ref, kg_ref,
                   o0_ref, o1_ref, o2_ref, lx_ref, lg_ref):
    x = x_ref[...]
    h = _rms_norm(x, gain_ref[...]).astype(BF16)

    def head_norm(t, gain):
        ms = jnp.dot((t * t).astype(BF16), hm_ref[...], preferred_element_type=F32)
        return t * lax.rsqrt(ms + RMS_EPS) * gain

    q = jnp.dot(h, w_ref[:, 0:D_ATT_QKV], preferred_element_type=F32)
    qn = (head_norm(q, qg_ref[...]) * (1.0 / math.sqrt(HEAD_DIM))).astype(BF16)
    k = jnp.dot(h, w_ref[:, D_ATT_QKV:2 * D_ATT_QKV], preferred_element_type=F32)
    kn = head_norm(k, kg_ref[...]).astype(BF16)
    v = jnp.dot(h, w_ref[:, 2 * D_ATT_QKV:3 * D_ATT_QKV],
                preferred_element_type=F32).astype(BF16)
    outs = (o0_ref, o1_ref, o2_ref)
    for g in range(N_GROUPS):
        for p in range(N_PAIRS):
            c = g * D_ATT_OUT + p * HEAD_PAIR
            outs[g][p] = qn[:, c:c + HEAD_PAIR]
            outs[g][N_PAIRS + p] = kn[:, c:c + HEAD_PAIR]
            outs[g][2 * N_PAIRS + p] = v[:, c:c + HEAD_PAIR]
    c0 = 3 * D_ATT_QKV
    lx_ref[...] = jnp.dot(h, w_ref[:, c0:c0 + D_LRU], preferred_element_type=F32)
    lg_ref[...] = jnp.dot(h, w_ref[:, c0 + D_LRU:c0 + 2 * D_LRU],
                          preferred_element_type=F32)


def _inproj(x, gain, w_in, head_mean, q_gain, k_gain):
    n = x.shape[0]
    d_in = w_in.shape[1]
    qkv_shape = jax.ShapeDtypeStruct((3 * N_PAIRS, n, HEAD_PAIR), BF16)
    qkv_spec = pl.BlockSpec((3 * N_PAIRS, TM_PROJ, HEAD_PAIR), lambda i: (0, i, 0))
    lru_shape = jax.ShapeDtypeStruct((n, D_LRU), F32)
    lru_spec = pl.BlockSpec((TM_PROJ, D_LRU), lambda i: (i, 0))
    return pl.pallas_call(
        _inproj_kernel,
        out_shape=(qkv_shape, qkv_shape, qkv_shape, lru_shape, lru_shape),
        grid=(n // TM_PROJ,),
        in_specs=[
            pl.BlockSpec((TM_PROJ, D_MODEL), lambda i: (i, 0)),
            _resident((1, D_MODEL), lambda i: (0, 0)),
            _resident((D_MODEL, d_in), lambda i: (0, 0)),
            _resident((D_ATT_QKV, D_ATT_QKV), lambda i: (0, 0)),
            _resident((1, D_ATT_QKV), lambda i: (0, 0)),
            _resident((1, D_ATT_QKV), lambda i: (0, 0)),
        ],
        out_specs=(qkv_spec, qkv_spec, qkv_spec, lru_spec, lru_spec),
        compiler_params=pltpu.CompilerParams(
            dimension_semantics=("arbitrary",), vmem_limit_bytes=VMEM_LIMIT),
        name="inproj",
    )(x, gain, w_in, head_mean, q_gain, k_gain)


def _attn_kernel(q0_ref, k0_ref, v0_ref, q1_ref, k1_ref, v1_ref,
                 q2_ref, k2_ref, v2_ref, bias_ref, out_ref,
                 so0, so1, so2, sl0, sl1, sl2):
    qkv = ((q0_ref, k0_ref, v0_ref), (q1_ref, k1_ref, v1_ref), (q2_ref, k2_ref, v2_ref))
    so = (so0, so1, so2)
    sl = (sl0, sl1, sl2)
    lane = lax.broadcasted_iota(jnp.int32, (ATT_BLOCK, HEAD_PAIR), 1)
    head0 = lane < HEAD_DIM

    def block(g, c, q_start, k_start, first):
        q_ref, k_ref, v_ref = qkv[g]
        cols = slice(c * HEAD_PAIR, (c + 1) * HEAD_PAIR)
        q = q_ref[pl.ds(q_start, ATT_BLOCK), cols]
        k = k_ref[pl.ds(k_start, 2 * ATT_BLOCK), cols]
        v = v_ref[pl.ds(k_start, 2 * ATT_BLOCK), cols]
        zero = jnp.zeros_like(q)
        o_heads, lse_heads = [], []
        for hh in range(2):
            qh = jnp.where(head0, q, zero) if hh == 0 else jnp.where(head0, zero, q)
            s = lax.dot_general(qh, k, (((1,), (1,)), ((), ())),
                                preferred_element_type=F32)
            s = s + bias_ref[g, hh, first]
            m = jnp.max(s, axis=-1, keepdims=True)
            e = jnp.exp(s - m)
            l = jnp.sum(e, axis=-1, keepdims=True)
            o = jnp.dot(e.astype(BF16), v, preferred_element_type=F32)
            o_heads.append(o / l)
            lse_heads.append(jnp.broadcast_to(m + jnp.log(l), (ATT_BLOCK, HEAD_PAIR)))
        so[g][c, pl.ds(q_start, ATT_BLOCK), :] = jnp.where(head0, o_heads[0], o_heads[1])
        sl[g][c, pl.ds(q_start, ATT_BLOCK), :] = jnp.where(head0, lse_heads[0], lse_heads[1])

    for g, (window, dil) in enumerate(ATT_GROUPS):
        n_blocks = so[g].shape[1] // ATT_BLOCK
        for c in range(dil):
            block(g, c, 0, 0, 1)

            def body(b, carry, g=g, c=c):
                k_start = pl.multiple_of((b - 1) * ATT_BLOCK, ATT_BLOCK)
                q_start = pl.multiple_of(b * ATT_BLOCK, ATT_BLOCK)
                block(g, c, q_start, k_start, 0)
                return carry

            lax.fori_loop(1, n_blocks, body, 0)

    rows = out_ref.shape[0]
    for c16 in range(MAX_DIL):
        os_, ls_ = [], []
        for g, (window, dil) in enumerate(ATT_GROUPS):
            stride = MAX_DIL // dil
            if stride == 1:
                os_.append(so[g][c16])
                ls_.append(sl[g][c16])
            else:
                idx = pl.ds(c16 // dil, rows, stride=stride)
                os_.append(so[g][c16 % dil, idx, :])
                ls_.append(sl[g][c16 % dil, idx, :])
        m = jnp.maximum(jnp.maximum(ls_[0], ls_[1]), ls_[2])
        es = [jnp.exp(l - m) for l in ls_]
        num = es[0] * os_[0] + es[1] * os_[1] + es[2] * os_[2]
        out_ref[:, c16 * HEAD_PAIR:(c16 + 1) * HEAD_PAIR] = num / (es[0] + es[1] + es[2])


def _attention(qkv_groups, bias, batch, seq):
    in_specs, args, scratch_o = [], [], []
    for g, (window, dil) in enumerate(ATT_GROUPS):
        arr = qkv_groups[g].reshape(3 * N_PAIRS, batch, seq // dil, dil * HEAD_PAIR)
        for part in range(3):
            in_specs.append(pl.BlockSpec(
                (None, None, seq // dil, dil * HEAD_PAIR),
                lambda b, p, part=part: (part * N_PAIRS + p, b, 0, 0)))
            args.append(arr)
        scratch_o.append(pltpu.VMEM((dil, seq // dil, HEAD_PAIR), F32))
    in_specs.append(pl.BlockSpec(
        (N_GROUPS, 2, 2, ATT_BLOCK, 2 * ATT_BLOCK), lambda b, p: (0, p, 0, 0, 0)))
    args.append(bias)
    out = pl.pallas_call(
        _attn_kernel,
        out_shape=jax.ShapeDtypeStruct(
            (N_PAIRS, batch, seq // MAX_DIL, MAX_DIL * HEAD_PAIR), F32),
        grid=(batch, N_PAIRS),
        in_specs=in_specs,
        out_specs=pl.BlockSpec((None, None, seq // MAX_DIL, MAX_DIL * HEAD_PAIR),
                               lambda b, p: (p, b, 0, 0)),
        scratch_shapes=scratch_o + scratch_o,
        compiler_params=pltpu.CompilerParams(
            dimension_semantics=("arbitrary", "arbitrary"), vmem_limit_bytes=VMEM_LIMIT),
        name="dilated_attention",
    )(*args)
    return out.reshape(N_PAIRS, batch * seq, HEAD_PAIR)


def _attention_bias():
    n = ATT_BLOCK
    heads = jnp.arange(1, N_QKV_HEADS + 1, dtype=F32)
    slopes = jnp.exp2(-ALIBI_MAX_BIAS * heads / N_QKV_HEADS).reshape(N_GROUPS, HEADS_PER_GROUP)
    qi = jnp.arange(n)[:, None]
    ki = jnp.arange(2 * n)[None, :]
    tables = []
    for g, (window, dil) in enumerate(ATT_GROUPS):
        steps = n + qi - ki
        valid = (steps >= 0) & (steps <= n)
        steps_f = qi - ki
        valid_f = steps_f >= 0
        per_head = []
        for hh in range(HEADS_PER_GROUP):
            sl = slopes[g, hh]
            reg = jnp.where(valid, -sl * (dil * steps).astype(F32), NEG_INF)
            fst = jnp.where(valid_f, -sl * (dil * steps_f).astype(F32), NEG_INF)
            per_head.append(jnp.stack([reg, fst]))
        tables.append(jnp.stack(per_head))
    return jnp.stack(tables).astype(F32)


def _lru_kernel(x_ref, gate_ref, cw_ref, cb_ref, wg_ref, ba_ref, bx_ref, lam_ref,
                y_ref, xs_ref, a_ref, b_ref, h_ref):
    t_idx = pl.program_id(0)
    n_batch = x_ref.shape[0]

    @pl.when(t_idx == 0)
    def _():
        xs_ref[:, 0:SUBLANES, :] = jnp.zeros((n_batch, SUBLANES, D_LRU), F32)
        h_ref[...] = jnp.zeros(h_ref.shape, F32)

    xs_ref[:, SUBLANES:, :] = x_ref[...]

    z = -lam_ref[...]
    softplus = jnp.maximum(z, 0.0) + jnp.log1p(jnp.exp(-jnp.abs(z)))
    decay = -LRU_C * softplus
    for bb in range(n_batch):
        xc = cb_ref[...]
        for w in range(CONV_WIDTH):
            start = SUBLANES - (CONV_WIDTH - 1) + w
            xc = xc + cw_ref[w:w + 1, :] * xs_ref[bb, pl.ds(start, LRU_TS), :]
        xcb = xc.astype(BF16)
        gates = [jnp.dot(xcb[:, i * LRU_GATE_BLK:(i + 1) * LRU_GATE_BLK], wg_ref[i],
                         preferred_element_type=F32)
                 for i in range(D_LRU // LRU_GATE_BLK)]
        ga = jnp.concatenate([t[:, :LRU_GATE_BLK] for t in gates], axis=-1) + ba_ref[...]
        gx = jnp.concatenate([t[:, LRU_GATE_BLK:] for t in gates], axis=-1) + bx_ref[...]
        r = 1.0 / (1.0 + jnp.exp(-ga))
        ig = 1.0 / (1.0 + jnp.exp(-gx))
        log_a = decay * r
        a = jnp.exp(log_a)
        b = jnp.sqrt(1.0 - a * a) * (ig * xc)
        for lt in range(N_LRU_LANE_TILES):
            cols = slice(lt * LANES, (lt + 1) * LANES)
            a_ref[lt, pl.ds(bb * LRU_PITCH, LRU_TS), :] = a[:, cols]
            b_ref[lt, pl.ds(bb * LRU_PITCH, LRU_TS), :] = b[:, cols]

    xs_ref[:, 0:SUBLANES, :] = xs_ref[:, LRU_TS:LRU_TS + SUBLANES, :]

    def step(t, hs):
        new = []
        for lt in range(N_LRU_LANE_TILES):
            idx = pl.ds(t, n_batch, stride=LRU_PITCH)
            h = a_ref[lt, idx, :] * hs[lt] + b_ref[lt, idx, :]
            b_ref[lt, idx, :] = h
            new.append(h)
        return tuple(new)

    hs = lax.fori_loop(0, LRU_TS, step,
                       tuple(h_ref[lt] for lt in range(N_LRU_LANE_TILES)))
    for lt in range(N_LRU_LANE_TILES):
        h_ref[lt] = hs[lt]

    for bb in range(n_batch):
        h = jnp.concatenate(
            [b_ref[lt, pl.ds(bb * LRU_PITCH, LRU_TS), :] for lt in range(N_LRU_LANE_TILES)],
            axis=-1)
        gt = gate_ref[bb]
        inner = math.sqrt(2.0 / math.pi) * (gt + 0.044715 * (gt * gt * gt))
        gelu = 0.5 * gt * (1.0 + jnp.tanh(inner))
        y_ref[bb] = (h * gelu).astype(y_ref.dtype)


def _lru(lru_x, lru_gate, conv_w, conv_b, w_gates, gate_a_b, gate_x_b, lam, batch, seq):
    x3 = lru_x.reshape(batch, seq, D_LRU)
    g3 = lru_gate.reshape(batch, seq, D_LRU)
    tile = pl.BlockSpec((batch, LRU_TS, D_LRU), lambda t: (0, t, 0))
    vec = _resident((1, D_LRU), lambda t: (0, 0))
    slab = pltpu.VMEM((N_LRU_LANE_TILES, batch * LRU_PITCH, LANES), F32)
    y = pl.pallas_call(
        _lru_kernel,
        out_shape=jax.ShapeDtypeStruct((batch, seq, D_LRU), BF16),
        grid=(seq // LRU_TS,),
        in_specs=[
            tile, tile,
            _resident((CONV_WIDTH, D_LRU), lambda t: (0, 0)),
            vec,
            _resident(w_gates.shape, lambda t: (0, 0, 0)),
            vec, vec, vec,
        ],
        out_specs=tile,
        scratch_shapes=[
            pltpu.VMEM((batch, LRU_TS + SUBLANES, D_LRU), F32),
            slab, slab,
            pltpu.VMEM((N_LRU_LANE_TILES, batch, LANES), F32),
        ],
        compiler_params=pltpu.CompilerParams(
            dimension_semantics=("arbitrary",), vmem_limit_bytes=VMEM_LIMIT),
        name="rg_lru",
    )(x3, g3, conv_w, conv_b, w_gates, gate_a_b, gate_x_b, lam)
    return y.reshape(batch * seq, D_LRU)


def _outproj_kernel(x_ref, att_ref, y_ref, w_ref, o_ref):
    mix = jnp.concatenate(
        [att_ref[p].astype(BF16) for p in range(N_PAIRS)] + [y_ref[...]], axis=-1)
    o_ref[...] = x_ref[...] + jnp.dot(mix, w_ref[...], preferred_element_type=F32)


def _outproj(x, att, y, w_out):
    n = x.shape[0]
    return pl.pallas_call(
        _outproj_kernel,
        out_shape=jax.ShapeDtypeStruct((n, D_MODEL), F32),
        grid=(n // TM_PROJ,),
        in_specs=[
            pl.BlockSpec((TM_PROJ, D_MODEL), lambda i: (i, 0)),
            pl.BlockSpec((N_PAIRS, TM_PROJ, HEAD_PAIR), lambda i: (0, i, 0)),
            pl.BlockSpec((TM_PROJ, D_LRU), lambda i: (i, 0)),
            _resident((D_ATT_OUT + D_LRU, D_MODEL), lambda i: (0, 0)),
        ],
        out_specs=pl.BlockSpec((TM_PROJ, D_MODEL), lambda i: (i, 0)),
        compiler_params=pltpu.CompilerParams(
            dimension_semantics=("arbitrary",), vmem_limit_bytes=VMEM_LIMIT),
        name="outproj",
    )(x, att, y, w_out)


def _prep_gate_up(w_gate, w_up):
    depth = w_gate.shape[0]

    def chunked(w):
        return w.reshape(depth, D_MODEL, N_FF_CHUNKS, FF_CHUNK).transpose(0, 2, 1, 3)

    return jnp.concatenate([chunked(w_gate), chunked(w_up)], axis=-1).astype(BF16)


def _prep_down(w_down):
    return w_down.reshape(w_down.shape[0], N_FF_CHUNKS, FF_CHUNK, D_MODEL).astype(BF16)


def _block_diag(w):
    depth = w.shape[0]
    per = LRU_GATE_BLK // LRU_BLOCK
    w = w.reshape(depth, D_LRU // LRU_GATE_BLK, per, LRU_BLOCK, LRU_BLOCK)
    eye = jnp.eye(per, dtype=w.dtype)
    full = w[:, :, :, :, None, :] * eye[None, None, :, None, :, None]
    return full.reshape(depth, D_LRU // LRU_GATE_BLK, LRU_GATE_BLK, LRU_GATE_BLK)


def kernel(x, ffn1_norm, ffn1_w_gate, ffn1_w_up, ffn1_w_down, mix_norm, w_in, q_norm, k_norm, conv_w, conv_b, gate_a_w, gate_a_b, gate_x_w, gate_x_b, lru_lambda, w_out, ffn2_norm, ffn2_w_gate, ffn2_w_up, ffn2_w_down):
    batch, seq, _ = x.shape
    depth = w_in.shape[0]
    n = batch * seq

    wgu1 = _prep_gate_up(ffn1_w_gate, ffn1_w_up)
    wd1 = _prep_down(ffn1_w_down)
    wgu2 = _prep_gate_up(ffn2_w_gate, ffn2_w_up)
    wd2 = _prep_down(ffn2_w_down)
    w_in_b = w_in.astype(BF16)
    w_out_b = w_out.astype(BF16)
    w_gates = jnp.concatenate([_block_diag(gate_a_w), _block_diag(gate_x_w)],
                              axis=-1).astype(BF16)
    head_id = jnp.arange(D_ATT_QKV) // HEAD_DIM
    head_mean = ((head_id[:, None] == head_id[None, :]).astype(F32) / HEAD_DIM).astype(BF16)
    q_gain = jnp.tile(q_norm, (1, N_QKV_HEADS)).reshape(depth, 1, D_ATT_QKV)
    k_gain = jnp.tile(k_norm, (1, N_QKV_HEADS)).reshape(depth, 1, D_ATT_QKV)
    bias = _attention_bias()

    def row(v, l):
        return v[l].reshape(1, -1)

    xf = x.reshape(n, D_MODEL)
    for l in range(depth):
        xf = _ffn(xf, row(ffn1_norm, l), wgu1[l], wd1[l])
        qkv0, qkv1, qkv2, lru_x, lru_gate = _inproj(
            xf, row(mix_norm, l), w_in_b[l], head_mean, q_gain[l], k_gain[l])
        att = _attention((qkv0, qkv1, qkv2), bias, batch, seq)
        y = _lru(lru_x, lru_gate, conv_w[l], row(conv_b, l), w_gates[l],
                 row(gate_a_b, l), row(gate_x_b, l), row(lru_lambda, l), batch, seq)
        xf = _outproj(xf, att, y, w_out_b[l])
        xf = _ffn(xf, row(ffn2_norm, l), wgu2[l], wd2[l])
    return xf.reshape(batch, seq, D_MODEL)
```

```python
import functools
import math

import jax
import jax.numpy as jnp
from jax import lax
from jax.experimental import pallas as pl
from jax.experimental.pallas import tpu as pltpu

F32 = jnp.float32
BF16 = jnp.bfloat16

D_MODEL = 1024
HEAD_DIM = 64
ATT_GROUPS = ((128, 1), (512, 4), (2048, 16))
N_GROUPS = len(ATT_GROUPS)
HEADS_PER_GROUP = 4
N_QKV_HEADS = N_GROUPS * HEADS_PER_GROUP
D_ATT_QKV = N_QKV_HEADS * HEAD_DIM
D_ATT_OUT = HEADS_PER_GROUP * HEAD_DIM
ALIBI_MAX_BIAS = 8.0
D_LRU = 768
LRU_BLOCK = 64
CONV_WIDTH = 4
LRU_C = 8.0
D_FF = 2816
FFN_RES_WEIGHT = 0.5
RMS_EPS = 1e-6
NEG_INF = -1e30

LANES = 128
SUBLANES = 8
MXU_DIM = 256
VMEM_LIMIT = 56 * 1024 * 1024

FF_CHUNK = MXU_DIM
N_FF_CHUNKS = D_FF // FF_CHUNK
TM_FFN = 512
TM_PROJ = 512
ATT_BLOCK = 128
HEAD_PAIR = LANES
N_PAIRS = D_ATT_OUT // HEAD_PAIR
MAX_DIL = ATT_GROUPS[-1][1]
ATT_UNROLL = 16
LRU_TS = 128
LRU_PITCH = LRU_TS + SUBLANES
LRU_GATE_BLK = MXU_DIM
N_LRU_LANE_TILES = D_LRU // LANES


def _resident(block_shape, index_map):
    return pl.BlockSpec(block_shape, index_map, pipeline_mode=pl.Buffered(1))


def _rms_norm(x, gain):
    ms = jnp.mean(x * x, axis=-1, keepdims=True)
    return x * lax.rsqrt(ms + RMS_EPS) * gain


def _ffn_kernel(x_ref, gain_ref, wg_ref, wu_ref, wd_ref, o_ref, acc_ref):
    x = x_ref[...]
    h = _rms_norm(x, gain_ref[...]).astype(BF16)
    for j in range(N_FF_CHUNKS):
        cols = slice(j * FF_CHUNK, (j + 1) * FF_CHUNK)
        g = jnp.dot(h, wg_ref[:, cols], preferred_element_type=F32)
        u = jnp.dot(h, wu_ref[:, cols], preferred_element_type=F32)
        a = (g / (1.0 + jnp.exp(-g)) * u).astype(BF16)
        d = jnp.dot(a, wd_ref[cols, :], preferred_element_type=F32)
        if j == 0:
            acc_ref[...] = d
        else:
            acc_ref[...] += d
    o_ref[...] = x + FFN_RES_WEIGHT * acc_ref[...]


def _ffn(x, gain, wg, wu, wd):
    n = x.shape[0]
    return pl.pallas_call(
        _ffn_kernel,
        out_shape=jax.ShapeDtypeStruct((n, D_MODEL), F32),
        grid=(n // TM_FFN,),
        in_specs=[
            pl.BlockSpec((TM_FFN, D_MODEL), lambda i: (i, 0)),
            _resident((1, D_MODEL), lambda i: (0, 0)),
            _resident((D_MODEL, D_FF), lambda i: (0, 0)),
            _resident((D_MODEL, D_FF), lambda i: (0, 0)),
            _resident((D_FF, D_MODEL), lambda i: (0, 0)),
        ],
        out_specs=pl.BlockSpec((TM_FFN, D_MODEL), lambda i: (i, 0)),
        scratch_shapes=[pltpu.VMEM((TM_FFN, D_MODEL), F32)],
        compiler_params=pltpu.CompilerParams(
            dimension_semantics=("arbitrary",), vmem_limit_bytes=VMEM_LIMIT),
        name="ffn",
    )(x, gain, wg, wu, wd)


def _inproj_kernel(x_ref, gain_ref, w_ref, hm_ref, qg_ref, kg_ref,
                   o0_ref, o1_ref, o2_ref, lx_ref, lg_ref, slab_ref):
    x = x_ref[...]
    h = _rms_norm(x, gain_ref[...]).astype(BF16)

    def head_norm(t, gain):
        ms = jnp.dot((t * t).astype(BF16), hm_ref[...], preferred_element_type=F32)
        return t * lax.rsqrt(ms + RMS_EPS) * gain

    q = jnp.dot(h, w_ref[:, 0:D_ATT_QKV], preferred_element_type=F32)
    qn = head_norm(q, qg_ref[...]) * (1.0 / math.sqrt(HEAD_DIM))
    k = jnp.dot(h, w_ref[:, D_ATT_QKV:2 * D_ATT_QKV], preferred_element_type=F32)
    kn = head_norm(k, kg_ref[...])
    v = jnp.dot(h, w_ref[:, 2 * D_ATT_QKV:3 * D_ATT_QKV], preferred_element_type=F32)
    outs = (o0_ref, o1_ref, o2_ref)
    slab = 0
    for g, (window, dil) in enumerate(ATT_GROUPS):
        for part, val in enumerate((qn, kn, v)):
            for p in range(N_PAIRS):
                c = g * D_ATT_OUT + p * HEAD_PAIR
                piece = val[:, c:c + HEAD_PAIR]
                dst = outs[g].at[part * N_PAIRS + p]
                if dil == 1:
                    dst[0] = piece.astype(BF16)
                    continue
                slab_ref[slab] = piece
                for cls in range(dil):
                    rows = pl.ds(cls, TM_PROJ // dil, stride=dil)
                    dst[cls] = slab_ref[slab, rows, :].astype(BF16)
                slab += 1
    c0 = 3 * D_ATT_QKV
    lx_ref[...] = jnp.dot(h, w_ref[:, c0:c0 + D_LRU], preferred_element_type=F32)
    lg_ref[...] = jnp.dot(h, w_ref[:, c0 + D_LRU:c0 + 2 * D_LRU],
                          preferred_element_type=F32)


def _inproj(x, gain, w_in, head_mean, q_gain, k_gain, batch, seq):
    n = x.shape[0]
    d_in = w_in.shape[1]
    tiles_per_seq = seq // TM_PROJ
    qkv_shapes, qkv_specs = [], []
    for window, dil in ATT_GROUPS:
        qkv_shapes.append(jax.ShapeDtypeStruct(
            (3 * N_PAIRS, batch, dil, seq // dil, HEAD_PAIR), BF16))
        qkv_specs.append(pl.BlockSpec(
            (3 * N_PAIRS, None, dil, TM_PROJ // dil, HEAD_PAIR),
            lambda i: (0, i // tiles_per_seq, 0, i % tiles_per_seq, 0)))
    n_slabs = 3 * N_PAIRS * sum(1 for _, dil in ATT_GROUPS if dil > 1)
    lru_shape = jax.ShapeDtypeStruct((n, D_LRU), F32)
    lru_spec = pl.BlockSpec((TM_PROJ, D_LRU), lambda i: (i, 0))
    return pl.pallas_call(
        _inproj_kernel,
        out_shape=(*qkv_shapes, lru_shape, lru_shape),
        grid=(n // TM_PROJ,),
        in_specs=[
            pl.BlockSpec((TM_PROJ, D_MODEL), lambda i: (i, 0)),
            _resident((1, D_MODEL), lambda i: (0, 0)),
            _resident((D_MODEL, d_in), lambda i: (0, 0)),
            _resident((D_ATT_QKV, D_ATT_QKV), lambda i: (0, 0)),
            _resident((1, D_ATT_QKV), lambda i: (0, 0)),
            _resident((1, D_ATT_QKV), lambda i: (0, 0)),
        ],
        out_specs=(*qkv_specs, lru_spec, lru_spec),
        scratch_shapes=[pltpu.VMEM((n_slabs, TM_PROJ, HEAD_PAIR), F32)],
        compiler_params=pltpu.CompilerParams(
            dimension_semantics=("arbitrary",), vmem_limit_bytes=VMEM_LIMIT),
        name="inproj",
    )(x, gain, w_in, head_mean, q_gain, k_gain)


def _attn_kernel(q0_ref, k0_ref, v0_ref, q1_ref, k1_ref, v1_ref,
                 q2_ref, k2_ref, v2_ref, bias_ref, out_ref, so_ref, sl_ref):
    qkv = ((q0_ref, k0_ref, v0_ref), (q1_ref, k1_ref, v1_ref), (q2_ref, k2_ref, v2_ref))
    seq = out_ref.shape[0]
    n_blocks = seq // ATT_BLOCK
    lane = lax.broadcasted_iota(jnp.int32, (ATT_BLOCK, HEAD_PAIR), 1)
    head0 = lane < HEAD_DIM

    def block(g, idx, blocks_per_class):
        q_ref, k_ref, v_ref = qkv[g]
        first = (idx & (blocks_per_class - 1)) == 0
        q_start = pl.multiple_of(idx * ATT_BLOCK, ATT_BLOCK)
        k_start = pl.multiple_of(jnp.where(first, idx, idx - 1) * ATT_BLOCK, ATT_BLOCK)
        table = first.astype(jnp.int32)
        q = q_ref[pl.ds(q_start, ATT_BLOCK), :]
        k = k_ref[pl.ds(k_start, 2 * ATT_BLOCK), :]
        v = v_ref[pl.ds(k_start, 2 * ATT_BLOCK), :]
        zero = jnp.zeros_like(q)
        q2 = jnp.concatenate([jnp.where(head0, q, zero), jnp.where(head0, zero, q)], axis=0)
        s2 = lax.dot_general(q2, k, (((1,), (1,)), ((), ())), preferred_element_type=F32)
        es, ms, ls = [], [], []
        for hh in range(2):
            s = s2[hh * ATT_BLOCK:(hh + 1) * ATT_BLOCK] + bias_ref[g, hh, table]
            m = jnp.max(s, axis=-1, keepdims=True)
            e = jnp.exp(s - m)
            ls.append(jnp.sum(e, axis=-1, keepdims=True))
            ms.append(m)
            es.append(e.astype(BF16))
        o2 = jnp.dot(jnp.concatenate(es, axis=0), v, preferred_element_type=F32)
        l_pair = jnp.where(head0, ls[0], ls[1])
        m_pair = jnp.where(head0, ms[0], ms[1])
        o_pair = jnp.where(head0, o2[:ATT_BLOCK], o2[ATT_BLOCK:])
        so_ref[g, pl.ds(q_start, ATT_BLOCK), :] = o_pair / l_pair
        sl_ref[g, pl.ds(q_start, ATT_BLOCK), :] = m_pair + jnp.log(l_pair)

    for g, (window, dil) in enumerate(ATT_GROUPS):
        blocks_per_class = n_blocks // dil

        def body(it, carry, g=g, blocks_per_class=blocks_per_class):
            for u in range(ATT_UNROLL):
                block(g, it * ATT_UNROLL + u, blocks_per_class)
            return carry

        lax.fori_loop(0, n_blocks // ATT_UNROLL, body, 0)

    rows = seq // MAX_DIL
    for c16 in range(MAX_DIL):
        os_, ls_ = [], []
        for g, (window, dil) in enumerate(ATT_GROUPS):
            stride = MAX_DIL // dil
            start = (c16 % dil) * (seq // dil) + c16 // dil
            idx = pl.ds(start, rows) if stride == 1 else pl.ds(start, rows, stride=stride)
            os_.append(so_ref[g, idx, :])
            ls_.append(sl_ref[g, idx, :])
        m = jnp.maximum(jnp.maximum(ls_[0], ls_[1]), ls_[2])
        es = [jnp.exp(l - m) for l in ls_]
        num = es[0] * os_[0] + es[1] * os_[1] + es[2] * os_[2]
        out_ref[pl.ds(c16, rows, stride=MAX_DIL), :] = num / (es[0] + es[1] + es[2])


def _attention(qkv_groups, bias, batch, seq):
    in_specs, args = [], []
    for g, (window, dil) in enumerate(ATT_GROUPS):
        arr = qkv_groups[g].reshape(3 * N_PAIRS, batch, seq, HEAD_PAIR)
        for part in range(3):
            in_specs.append(pl.BlockSpec(
                (None, None, seq, HEAD_PAIR),
                lambda b, p, part=part: (part * N_PAIRS + p, b, 0, 0)))
            args.append(arr)
    in_specs.append(pl.BlockSpec(
        (N_GROUPS, 2, 2, ATT_BLOCK, 2 * ATT_BLOCK), lambda b, p: (0, p, 0, 0, 0)))
    args.append(bias)
    scratch = pltpu.VMEM((N_GROUPS, seq, HEAD_PAIR), F32)
    out = pl.pallas_call(
        _attn_kernel,
        out_shape=jax.ShapeDtypeStruct((N_PAIRS, batch, seq, HEAD_PAIR), F32),
        grid=(batch, N_PAIRS),
        in_specs=in_specs,
        out_specs=pl.BlockSpec((None, None, seq, HEAD_PAIR), lambda b, p: (p, b, 0, 0)),
        scratch_shapes=[scratch, scratch],
        compiler_params=pltpu.CompilerParams(
            dimension_semantics=("arbitrary", "arbitrary"), vmem_limit_bytes=VMEM_LIMIT),
        name="dilated_attention",
    )(*args)
    return out.reshape(N_PAIRS, batch * seq, HEAD_PAIR)


def _attention_bias():
    n = ATT_BLOCK
    heads = jnp.arange(1, N_QKV_HEADS + 1, dtype=F32)
    slopes = jnp.exp2(-ALIBI_MAX_BIAS * heads / N_QKV_HEADS).reshape(N_GROUPS, HEADS_PER_GROUP)
    qi = jnp.arange(n)[:, None]
    ki = jnp.arange(2 * n)[None, :]
    tables = []
    for g, (window, dil) in enumerate(ATT_GROUPS):
        steps = n + qi - ki
        valid = (steps >= 0) & (steps <= n)
        steps_f = qi - ki
        valid_f = steps_f >= 0
        per_head = []
        for hh in range(HEADS_PER_GROUP):
            sl = slopes[g, hh]
            reg = jnp.where(valid, -sl * (dil * steps).astype(F32), NEG_INF)
            fst = jnp.where(valid_f, -sl * (dil * steps_f).astype(F32), NEG_INF)
            per_head.append(jnp.stack([reg, fst]))
        tables.append(jnp.stack(per_head))
    return jnp.stack(tables).astype(F32)


def _lru_kernel(x_ref, gate_ref, cw_ref, cb_ref, wg_ref, ba_ref, bx_ref, lam_ref,
                y_ref, xs_ref, a_ref, b_ref, h_ref):
    t_idx = pl.program_id(0)
    n_batch = x_ref.shape[0]

    @pl.when(t_idx == 0)
    def _():
        xs_ref[:, 0:SUBLANES, :] = jnp.zeros((n_batch, SUBLANES, D_LRU), F32)
        h_ref[...] = jnp.zeros(h_ref.shape, F32)

    xs_ref[:, SUBLANES:, :] = x_ref[...]

    z = -lam_ref[...]
    softplus = jnp.maximum(z, 0.0) + jnp.log1p(jnp.exp(-jnp.abs(z)))
    decay = -LRU_C * softplus
    for bb in range(n_batch):
        xc = cb_ref[...]
        for w in range(CONV_WIDTH):
            start = SUBLANES - (CONV_WIDTH - 1) + w
            xc = xc + cw_ref[w:w + 1, :] * xs_ref[bb, pl.ds(start, LRU_TS), :]
        xcb = xc.astype(BF16)
        gates = [jnp.dot(xcb[:, i * LRU_GATE_BLK:(i + 1) * LRU_GATE_BLK], wg_ref[i],
                         preferred_element_type=F32)
                 for i in range(D_LRU // LRU_GATE_BLK)]
        ga = jnp.concatenate([t[:, :LRU_GATE_BLK] for t in gates], axis=-1) + ba_ref[...]
        gx = jnp.concatenate([t[:, LRU_GATE_BLK:] for t in gates], axis=-1) + bx_ref[...]
        r = 1.0 / (1.0 + jnp.exp(-ga))
        ig = 1.0 / (1.0 + jnp.exp(-gx))
        log_a = decay * r
        a = jnp.exp(log_a)
        b = jnp.sqrt(1.0 - a * a) * (ig * xc)
        for lt in range(N_LRU_LANE_TILES):
            cols = slice(lt * LANES, (lt + 1) * LANES)
            a_ref[lt, pl.ds(bb * LRU_PITCH, LRU_TS), :] = a[:, cols]
            b_ref[lt, pl.ds(bb * LRU_PITCH, LRU_TS), :] = b[:, cols]

    xs_ref[:, 0:SUBLANES, :] = xs_ref[:, LRU_TS:LRU_TS + SUBLANES, :]

    def step(t, hs):
        new = []
        for lt in range(N_LRU_LANE_TILES):
            idx = pl.ds(t, n_batch, stride=LRU_PITCH)
            h = a_ref[lt, idx, :] * hs[lt] + b_ref[lt, idx, :]
            b_ref[lt, idx, :] = h
            new.append(h)
        return tuple(new)

    hs = lax.fori_loop(0, LRU_TS, step,
                       tuple(h_ref[lt] for lt in range(N_LRU_LANE_TILES)))
    for lt in range(N_LRU_LANE_TILES):
        h_ref[lt] = hs[lt]

    for bb in range(n_batch):
        h = jnp.concatenate(
            [b_ref[lt, pl.ds(bb * LRU_PITCH, LRU_TS), :] for lt in range(N_LRU_LANE_TILES)],
            axis=-1)
        gt = gate_ref[bb]
        inner = math.sqrt(2.0 / math.pi) * (gt + 0.044715 * (gt * gt * gt))
        gelu = 0.5 * gt * (1.0 + jnp.tanh(inner))
        y_ref[bb] = (h * gelu).astype(y_ref.dtype)


def _lru(lru_x, lru_gate, conv_w, conv_b, w_gates, gate_a_b, gate_x_b, lam, batch, seq):
    x3 = lru_x.reshape(batch, seq, D_LRU)
    g3 = lru_gate.reshape(batch, seq, D_LRU)
    tile = pl.BlockSpec((batch, LRU_TS, D_LRU), lambda t: (0, t, 0))
    vec = _resident((1, D_LRU), lambda t: (0, 0))
    slab = pltpu.VMEM((N_LRU_LANE_TILES, batch * LRU_PITCH, LANES), F32)
    y = pl.pallas_call(
        _lru_kernel,
        out_shape=jax.ShapeDtypeStruct((batch, seq, D_LRU), BF16),
        grid=(seq // LRU_TS,),
        in_specs=[
            tile, tile,
            _resident((CONV_WIDTH, D_LRU), lambda t: (0, 0)),
            vec,
            _resident(w_gates.shape, lambda t: (0, 0, 0)),
            vec, vec, vec,
        ],
        out_specs=tile,
        scratch_shapes=[
            pltpu.VMEM((batch, LRU_TS + SUBLANES, D_LRU), F32),
            slab, slab,
            pltpu.VMEM((N_LRU_LANE_TILES, batch, LANES), F32),
        ],
        compiler_params=pltpu.CompilerParams(
            dimension_semantics=("arbitrary",), vmem_limit_bytes=VMEM_LIMIT),
        name="rg_lru",
    )(x3, g3, conv_w, conv_b, w_gates, gate_a_b, gate_x_b, lam)
    return y.reshape(batch * seq, D_LRU)


def _outproj_kernel(x_ref, att_ref, y_ref, w_ref, o_ref):
    mix = jnp.concatenate(
        [att_ref[p].astype(BF16) for p in range(N_PAIRS)] + [y_ref[...]], axis=-1)
    o_ref[...] = x_ref[...] + jnp.dot(mix, w_ref[...], preferred_element_type=F32)


def _outproj(x, att, y, w_out):
    n = x.shape[0]
    return pl.pallas_call(
        _outproj_kernel,
        out_shape=jax.ShapeDtypeStruct((n, D_MODEL), F32),
        grid=(n // TM_PROJ,),
        in_specs=[
            pl.BlockSpec((TM_PROJ, D_MODEL), lambda i: (i, 0)),
            pl.BlockSpec((N_PAIRS, TM_PROJ, HEAD_PAIR), lambda i: (0, i, 0)),
            pl.BlockSpec((TM_PROJ, D_LRU), lambda i: (i, 0)),
            _resident((D_ATT_OUT + D_LRU, D_MODEL), lambda i: (0, 0)),
        ],
        out_specs=pl.BlockSpec((TM_PROJ, D_MODEL), lambda i: (i, 0)),
        compiler_params=pltpu.CompilerParams(
            dimension_semantics=("arbitrary",), vmem_limit_bytes=VMEM_LIMIT),
        name="outproj",
    )(x, att, y, w_out)


def _block_diag(w):
    depth = w.shape[0]
    per = LRU_GATE_BLK // LRU_BLOCK
    w = w.reshape(depth, D_LRU // LRU_GATE_BLK, per, LRU_BLOCK, LRU_BLOCK)
    eye = jnp.eye(per, dtype=w.dtype)
    full = w[:, :, :, :, None, :] * eye[None, None, :, None, :, None]
    return full.reshape(depth, D_LRU // LRU_GATE_BLK, LRU_GATE_BLK, LRU_GATE_BLK)


def kernel(x, ffn1_norm, ffn1_w_gate, ffn1_w_up, ffn1_w_down, mix_norm, w_in, q_norm, k_norm, conv_w, conv_b, gate_a_w, gate_a_b, gate_x_w, gate_x_b, lru_lambda, w_out, ffn2_norm, ffn2_w_gate, ffn2_w_up, ffn2_w_down):
    batch, seq, _ = x.shape
    depth = w_in.shape[0]
    n = batch * seq

    wg1, wu1, wd1 = (w.astype(BF16) for w in (ffn1_w_gate, ffn1_w_up, ffn1_w_down))
    wg2, wu2, wd2 = (w.astype(BF16) for w in (ffn2_w_gate, ffn2_w_up, ffn2_w_down))
    w_in_b = w_in.astype(BF16)
    w_out_b = w_out.astype(BF16)
    w_gates = jnp.concatenate([_block_diag(gate_a_w), _block_diag(gate_x_w)],
                              axis=-1).astype(BF16)
    head_id = jnp.arange(D_ATT_QKV) // HEAD_DIM
    head_mean = ((head_id[:, None] == head_id[None, :]).astype(F32) / HEAD_DIM).astype(BF16)
    q_gain = jnp.tile(q_norm, (1, N_QKV_HEADS)).reshape(depth, 1, D_ATT_QKV)
    k_gain = jnp.tile(k_norm, (1, N_QKV_HEADS)).reshape(depth, 1, D_ATT_QKV)
    bias = _attention_bias()

    def row(v, l):
        return v[l].reshape(1, -1)

    xf = x.reshape(n, D_MODEL)
    for l in range(depth):
        xf = _ffn(xf, row(ffn1_norm, l), wg1[l], wu1[l], wd1[l])
        qkv0, qkv1, qkv2, lru_x, lru_gate = _inproj(
            xf, row(mix_norm, l), w_in_b[l], head_mean, q_gain[l], k_gain[l], batch, seq)
        att = _attention((qkv0, qkv1, qkv2), bias, batch, seq)
        y = _lru(lru_x, lru_gate, conv_w[l], row(conv_b, l), w_gates[l],
                 row(gate_a_b, l), row(gate_x_b, l), row(lru_lambda, l), batch, seq)
        xf = _outproj(xf, att, y, w_out_b[l])
        xf = _ffn(xf, row(ffn2_norm, l), wg2[l], wu2[l], wd2[l])
    return xf.reshape(batch, seq, D_MODEL)
```

```python
import functools
import math

import jax
import jax.numpy as jnp
from jax import lax
from jax.experimental import pallas as pl
from jax.experimental.pallas import tpu as pltpu

F32 = jnp.float32
BF16 = jnp.bfloat16

D_MODEL = 1024
HEAD_DIM = 64
ATT_GROUPS = ((128, 1), (512, 4), (2048, 16))
N_GROUPS = len(ATT_GROUPS)
HEADS_PER_GROUP = 4
N_QKV_HEADS = N_GROUPS * HEADS_PER_GROUP
D_ATT_QKV = N_QKV_HEADS * HEAD_DIM
D_ATT_OUT = HEADS_PER_GROUP * HEAD_DIM
ALIBI_MAX_BIAS = 8.0
D_LRU = 768
LRU_BLOCK = 64
CONV_WIDTH = 4
LRU_C = 8.0
D_FF = 2816
FFN_RES_WEIGHT = 0.5
RMS_EPS = 1e-6
NEG_INF = -1e30

LANES = 128
SUBLANES = 8
MXU_DIM = 256
VMEM_LIMIT = 56 * 1024 * 1024

FF_CHUNK = MXU_DIM
N_FF_CHUNKS = D_FF // FF_CHUNK
TM_FFN = 512
TM_PROJ = 512
ATT_BLOCK = 128
HEAD_PAIR = LANES
N_PAIRS = D_ATT_OUT // HEAD_PAIR
MAX_DIL = ATT_GROUPS[-1][1]
ATT_UNROLL = 16
HEADS_PER_MXU = MXU_DIM // HEAD_DIM
LRU_GATE_BLK = MXU_DIM
N_LRU_LANE_TILES = D_LRU // LANES
LRU_SEG = TM_PROJ // SUBLANES
LRU_PITCH = LRU_SEG + SUBLANES


def _resident(block_shape, index_map):
    return pl.BlockSpec(block_shape, index_map, pipeline_mode=pl.Buffered(1))


def _rms_norm(x, gain):
    ms = jnp.mean(x * x, axis=-1, keepdims=True)
    return x * lax.rsqrt(ms + RMS_EPS) * gain


def _sigmoid(x):
    return 0.5 + 0.5 * jnp.tanh(0.5 * x)


def _ffn_kernel(*refs, with_mixer_out):
    if with_mixer_out:
        x_ref, att_ref, y_ref, wo_ref, gain_ref, wg_ref, wu_ref, wd_ref, o_ref, acc_ref = refs
        mix = jnp.concatenate(
            [att_ref[p].astype(BF16) for p in range(N_PAIRS)] + [y_ref[...]], axis=-1)
        x = x_ref[...] + jnp.dot(mix, wo_ref[...], preferred_element_type=F32)
    else:
        x_ref, gain_ref, wg_ref, wu_ref, wd_ref, o_ref, acc_ref = refs
        x = x_ref[...]
    h = _rms_norm(x, gain_ref[...]).astype(BF16)
    for j in range(N_FF_CHUNKS):
        cols = slice(j * FF_CHUNK, (j + 1) * FF_CHUNK)
        g = jnp.dot(h, wg_ref[:, cols], preferred_element_type=F32)
        u = jnp.dot(h, wu_ref[:, cols], preferred_element_type=F32)
        a = (g / (1.0 + jnp.exp(-g)) * u).astype(BF16)
        d = jnp.dot(a, wd_ref[cols, :], preferred_element_type=F32)
        if j == 0:
            acc_ref[...] = d
        else:
            acc_ref[...] += d
    o_ref[...] = x + FFN_RES_WEIGHT * acc_ref[...]


def _ffn(x, gain, wg, wu, wd, mixer_out=None):
    n = x.shape[0]
    row_tile = pl.BlockSpec((TM_FFN, D_MODEL), lambda i: (i, 0))
    in_specs, args = [row_tile], [x]
    if mixer_out is not None:
        att, y, w_out = mixer_out
        in_specs += [
            pl.BlockSpec((N_PAIRS, TM_FFN, HEAD_PAIR), lambda i: (0, i, 0)),
            pl.BlockSpec((TM_FFN, D_LRU), lambda i: (i, 0)),
            _resident((D_ATT_OUT + D_LRU, D_MODEL), lambda i: (0, 0)),
        ]
        args += [att, y, w_out]
    in_specs += [
        _resident((1, D_MODEL), lambda i: (0, 0)),
        _resident((D_MODEL, D_FF), lambda i: (0, 0)),
        _resident((D_MODEL, D_FF), lambda i: (0, 0)),
        _resident((D_FF, D_MODEL), lambda i: (0, 0)),
    ]
    args += [gain, wg, wu, wd]
    return pl.pallas_call(
        functools.partial(_ffn_kernel, with_mixer_out=mixer_out is not None),
        out_shape=jax.ShapeDtypeStruct((n, D_MODEL), F32),
        grid=(n // TM_FFN,),
        in_specs=in_specs,
        out_specs=row_tile,
        scratch_shapes=[pltpu.VMEM((TM_FFN, D_MODEL), F32)],
        compiler_params=pltpu.CompilerParams(
            dimension_semantics=("arbitrary",), vmem_limit_bytes=VMEM_LIMIT),
        name="ffn_mixer_out" if mixer_out is not None else "ffn",
    )(*args)


def _mixer_in_kernel(x_ref, gain_ref, w_ref, hm_ref, qg_ref, kg_ref,
                     cw_ref, cb_ref, wgate_ref, ba_ref, bx_ref, lam_ref,
                     o0_ref, o1_ref, o2_ref, y_ref,
                     slab_ref, xs_ref, a_ref, b_ref, h_ref):
    @pl.when(pl.program_id(1) == 0)
    def _():
        xs_ref[0:SUBLANES, :] = jnp.zeros((SUBLANES, D_LRU), F32)
        h_ref[...] = jnp.zeros(h_ref.shape, F32)

    x = x_ref[...]
    h = _rms_norm(x, gain_ref[...]).astype(BF16)

    def head_norm(t, gain):
        sq = (t * t).astype(BF16)
        ms = jnp.concatenate(
            [jnp.dot(sq[:, i * MXU_DIM:(i + 1) * MXU_DIM], hm_ref[...],
                     preferred_element_type=F32)
             for i in range(D_ATT_QKV // MXU_DIM)], axis=-1)
        return t * lax.rsqrt(ms + RMS_EPS) * gain

    c0 = 3 * D_ATT_QKV
    lru_x = jnp.dot(h, w_ref[:, c0:c0 + D_LRU], preferred_element_type=F32)
    gate = jnp.dot(h, w_ref[:, c0 + D_LRU:c0 + 2 * D_LRU], preferred_element_type=F32)

    xs_ref[SUBLANES:, :] = lru_x
    xc = cb_ref[...]
    for w in range(CONV_WIDTH):
        start = SUBLANES - (CONV_WIDTH - 1) + w
        xc = xc + cw_ref[w:w + 1, :] * xs_ref[pl.ds(start, TM_PROJ), :]
    xs_ref[0:SUBLANES, :] = lru_x[TM_PROJ - SUBLANES:, :]

    xcb = xc.astype(BF16)
    gates = [jnp.dot(xcb[:, i * LRU_GATE_BLK:(i + 1) * LRU_GATE_BLK], wgate_ref[i],
                     preferred_element_type=F32)
             for i in range(D_LRU // LRU_GATE_BLK)]
    ga = jnp.concatenate([t[:, :LRU_GATE_BLK] for t in gates], axis=-1) + ba_ref[...]
    gx = jnp.concatenate([t[:, LRU_GATE_BLK:] for t in gates], axis=-1) + bx_ref[...]
    z = -lam_ref[...]
    softplus = jnp.maximum(z, 0.0) + jnp.log1p(jnp.exp(-jnp.abs(z)))
    log_a = (-LRU_C * softplus) * _sigmoid(ga)
    a = jnp.exp(log_a)
    b = jnp.sqrt(1.0 - a * a) * (_sigmoid(gx) * xc)

    h_cols = []
    for lt in range(N_LRU_LANE_TILES):
        cols = slice(lt * LANES, (lt + 1) * LANES)
        for s in range(SUBLANES):
            rows = slice(s * LRU_SEG, (s + 1) * LRU_SEG)
            a_ref[lt, pl.ds(s * LRU_PITCH, LRU_SEG), :] = a[rows, cols]
            b_ref[lt, pl.ds(s * LRU_PITCH, LRU_SEG), :] = b[rows, cols]
        h_loc = jnp.zeros((SUBLANES, LANES), F32)
        a_cum = jnp.ones((SUBLANES, LANES), F32)
        for i in range(LRU_SEG):
            idx = pl.ds(i, SUBLANES, stride=LRU_PITCH)
            a_i = a_ref[lt, idx, :]
            h_loc = a_i * h_loc + b_ref[lt, idx, :]
            a_cum = a_cum * a_i
            b_ref[lt, idx, :] = h_loc
            a_ref[lt, idx, :] = a_cum
        carry = h_ref[lt, 0:1, :]
        segs = []
        for s in range(SUBLANES):
            rows = pl.ds(s * LRU_PITCH, LRU_SEG)
            segs.append(b_ref[lt, rows, :] + a_ref[lt, rows, :] * carry)
            carry = a_cum[s:s + 1, :] * carry + h_loc[s:s + 1, :]
        h_ref[lt, 0:1, :] = carry
        h_cols.append(jnp.concatenate(segs, axis=0))
    h_all = jnp.concatenate(h_cols, axis=-1)

    inner = math.sqrt(2.0 / math.pi) * (gate + 0.044715 * (gate * gate * gate))
    gelu = 0.5 * gate * (1.0 + jnp.tanh(inner))
    y_ref[...] = (h_all * gelu).astype(y_ref.dtype)

    q = jnp.dot(h, w_ref[:, 0:D_ATT_QKV], preferred_element_type=F32)
    qn = head_norm(q, qg_ref[...])
    k = jnp.dot(h, w_ref[:, D_ATT_QKV:2 * D_ATT_QKV], preferred_element_type=F32)
    kn = head_norm(k, kg_ref[...])
    v = jnp.dot(h, w_ref[:, 2 * D_ATT_QKV:3 * D_ATT_QKV], preferred_element_type=F32)
    outs = (o0_ref, o1_ref, o2_ref)
    slab = 0
    for g, (window, dil) in enumerate(ATT_GROUPS):
        for part, val in enumerate((qn, kn, v)):
            for p in range(N_PAIRS):
                c = g * D_ATT_OUT + p * HEAD_PAIR
                piece = val[:, c:c + HEAD_PAIR]
                dst = outs[g].at[part * N_PAIRS + p]
                if dil == 1:
                    dst[0] = piece.astype(BF16)
                    continue
                slab_ref[slab] = piece
                for cls in range(dil):
                    rows = pl.ds(cls, TM_PROJ // dil, stride=dil)
                    dst[cls] = slab_ref[slab, rows, :].astype(BF16)
                slab += 1


def _mixer_in(x, gain, w_in, head_mean, q_gain, k_gain, conv_w, conv_b, w_gates,
              gate_a_b, gate_x_b, lam, batch, seq):
    n = x.shape[0]
    d_in = w_in.shape[1]
    tiles_per_seq = seq // TM_PROJ

    def token_tile(width):
        return pl.BlockSpec((TM_PROJ, width), lambda b, t: (b * tiles_per_seq + t, 0))

    def const(shape):
        return _resident(shape, lambda b, t: (0,) * len(shape))

    qkv_shapes, qkv_specs = [], []
    for window, dil in ATT_GROUPS:
        qkv_shapes.append(jax.ShapeDtypeStruct(
            (3 * N_PAIRS, batch, dil, seq // dil, HEAD_PAIR), BF16))
        qkv_specs.append(pl.BlockSpec(
            (3 * N_PAIRS, None, dil, TM_PROJ // dil, HEAD_PAIR),
            lambda b, t: (0, b, 0, t, 0)))
    n_slabs = 3 * N_PAIRS * sum(1 for _, dil in ATT_GROUPS if dil > 1)
    scan_slab = pltpu.VMEM((N_LRU_LANE_TILES, SUBLANES * LRU_PITCH, LANES), F32)
    return pl.pallas_call(
        _mixer_in_kernel,
        out_shape=(*qkv_shapes, jax.ShapeDtypeStruct((n, D_LRU), BF16)),
        grid=(batch, tiles_per_seq),
        in_specs=[
            token_tile(D_MODEL),
            const((1, D_MODEL)),
            const((D_MODEL, d_in)),
            const((MXU_DIM, MXU_DIM)),
            const((1, D_ATT_QKV)),
            const((1, D_ATT_QKV)),
            const((CONV_WIDTH, D_LRU)),
            const((1, D_LRU)),
            const(w_gates.shape),
            const((1, D_LRU)),
            const((1, D_LRU)),
            const((1, D_LRU)),
        ],
        out_specs=(*qkv_specs, token_tile(D_LRU)),
        scratch_shapes=[
            pltpu.VMEM((n_slabs, TM_PROJ, HEAD_PAIR), F32),
            pltpu.VMEM((TM_PROJ + SUBLANES, D_LRU), F32),
            scan_slab, scan_slab,
            pltpu.VMEM((N_LRU_LANE_TILES, SUBLANES, LANES), F32),
        ],
        compiler_params=pltpu.CompilerParams(
            dimension_semantics=("arbitrary", "arbitrary"), vmem_limit_bytes=VMEM_LIMIT),
        name="mixer_in",
    )(x, gain, w_in, head_mean, q_gain, k_gain, conv_w, conv_b, w_gates,
      gate_a_b, gate_x_b, lam)


def _attn_kernel(q0_ref, k0_ref, v0_ref, q1_ref, k1_ref, v1_ref,
                 q2_ref, k2_ref, v2_ref, bias_ref, out_ref, so_ref, sl_ref):
    qkv = ((q0_ref, k0_ref, v0_ref), (q1_ref, k1_ref, v1_ref), (q2_ref, k2_ref, v2_ref))
    seq = out_ref.shape[0]
    n_blocks = seq // ATT_BLOCK
    lane = lax.broadcasted_iota(jnp.int32, (ATT_BLOCK, HEAD_PAIR), 1)
    head0 = lane < HEAD_DIM

    def block(g, idx, blocks_per_class):
        q_ref, k_ref, v_ref = qkv[g]
        first = (idx & (blocks_per_class - 1)) == 0
        q_start = pl.multiple_of(idx * ATT_BLOCK, ATT_BLOCK)
        k_start = pl.multiple_of(jnp.where(first, idx, idx - 1) * ATT_BLOCK, ATT_BLOCK)
        table = jnp.where(first, 1, 0)
        q = q_ref[pl.ds(q_start, ATT_BLOCK), :]
        k = k_ref[pl.ds(k_start, 2 * ATT_BLOCK), :]
        v = v_ref[pl.ds(k_start, 2 * ATT_BLOCK), :]
        zero = jnp.zeros_like(q)
        q2 = jnp.concatenate([jnp.where(head0, q, zero), jnp.where(head0, zero, q)], axis=0)
        s2 = lax.dot_general(q2, k, (((1,), (1,)), ((), ())), preferred_element_type=F32)
        es, ms, ls = [], [], []
        for hh in range(2):
            s = s2[hh * ATT_BLOCK:(hh + 1) * ATT_BLOCK] + bias_ref[g, hh, table]
            m = jnp.max(s, axis=-1, keepdims=True)
            e = jnp.exp(s - m)
            ls.append(jnp.sum(e, axis=-1, keepdims=True))
            ms.append(m)
            es.append(e.astype(BF16))
        o2 = jnp.dot(jnp.concatenate(es, axis=0), v, preferred_element_type=F32)
        l_pair = jnp.where(head0, ls[0], ls[1])
        m_pair = jnp.where(head0, ms[0], ms[1])
        o_pair = jnp.where(head0, o2[:ATT_BLOCK], o2[ATT_BLOCK:])
        so_ref[g, pl.ds(q_start, ATT_BLOCK), :] = o_pair / l_pair
        sl_ref[g, pl.ds(q_start, ATT_BLOCK), :] = m_pair + jnp.log(l_pair)

    for g, (window, dil) in enumerate(ATT_GROUPS):
        blocks_per_class = n_blocks // dil

        def body(it, carry, g=g, blocks_per_class=blocks_per_class):
            for u in range(ATT_UNROLL):
                block(g, it * ATT_UNROLL + u, blocks_per_class)
            return carry

        lax.fori_loop(0, n_blocks // ATT_UNROLL, body, 0)

    rows = seq // MAX_DIL
    for c16 in range(MAX_DIL):
        os_, ls_ = [], []
        for g, (window, dil) in enumerate(ATT_GROUPS):
            stride = MAX_DIL // dil
            start = (c16 % dil) * (seq // dil) + c16 // dil
            idx = pl.ds(start, rows) if stride == 1 else pl.ds(start, rows, stride=stride)
            os_.append(so_ref[g, idx, :])
            ls_.append(sl_ref[g, idx, :])
        m = jnp.maximum(jnp.maximum(ls_[0], ls_[1]), ls_[2])
        es = [jnp.exp(l - m) for l in ls_]
        num = es[0] * os_[0] + es[1] * os_[1] + es[2] * os_[2]
        out_ref[pl.ds(c16, rows, stride=MAX_DIL), :] = num / (es[0] + es[1] + es[2])


def _attention(qkv_groups, bias, batch, seq):
    in_specs, args = [], []
    for g, (window, dil) in enumerate(ATT_GROUPS):
        arr = qkv_groups[g].reshape(3 * N_PAIRS, batch, seq, HEAD_PAIR)
        for part in range(3):
            in_specs.append(pl.BlockSpec(
                (None, None, seq, HEAD_PAIR),
                lambda b, p, part=part: (part * N_PAIRS + p, b, 0, 0)))
            args.append(arr)
    in_specs.append(pl.BlockSpec(
        (N_GROUPS, 2, 2, ATT_BLOCK, 2 * ATT_BLOCK), lambda b, p: (0, p, 0, 0, 0)))
    args.append(bias)
    scratch = pltpu.VMEM((N_GROUPS, seq, HEAD_PAIR), F32)
    out = pl.pallas_call(
        _attn_kernel,
        out_shape=jax.ShapeDtypeStruct((N_PAIRS, batch, seq, HEAD_PAIR), F32),
        grid=(batch, N_PAIRS),
        in_specs=in_specs,
        out_specs=pl.BlockSpec((None, None, seq, HEAD_PAIR), lambda b, p: (p, b, 0, 0)),
        scratch_shapes=[scratch, scratch],
        compiler_params=pltpu.CompilerParams(
            dimension_semantics=("arbitrary", "arbitrary"), vmem_limit_bytes=VMEM_LIMIT),
        name="dilated_attention",
    )(*args)
    return out.reshape(N_PAIRS, batch * seq, HEAD_PAIR)


def _attention_bias():
    n = ATT_BLOCK
    heads = jnp.arange(1, N_QKV_HEADS + 1, dtype=F32)
    slopes = jnp.exp2(-ALIBI_MAX_BIAS * heads / N_QKV_HEADS).reshape(N_GROUPS, HEADS_PER_GROUP)
    qi = jnp.arange(n)[:, None]
    ki = jnp.arange(2 * n)[None, :]
    tables = []
    for g, (window, dil) in enumerate(ATT_GROUPS):
        steps = n + qi - ki
        valid = (steps >= 0) & (steps <= n)
        steps_f = qi - ki
        valid_f = steps_f >= 0
        per_head = []
        for hh in range(HEADS_PER_GROUP):
            sl = slopes[g, hh]
            reg = jnp.where(valid, -sl * (dil * steps).astype(F32), NEG_INF)
            fst = jnp.where(valid_f, -sl * (dil * steps_f).astype(F32), NEG_INF)
            per_head.append(jnp.stack([reg, fst]))
        tables.append(jnp.stack(per_head))
    return jnp.stack(tables).astype(F32)


def _block_diag(w):
    depth = w.shape[0]
    per = LRU_GATE_BLK // LRU_BLOCK
    w = w.reshape(depth, D_LRU // LRU_GATE_BLK, per, LRU_BLOCK, LRU_BLOCK)
    eye = jnp.eye(per, dtype=w.dtype)
    full = w[:, :, :, :, None, :] * eye[None, None, :, None, :, None]
    return full.reshape(depth, D_LRU // LRU_GATE_BLK, LRU_GATE_BLK, LRU_GATE_BLK)


def kernel(x, ffn1_norm, ffn1_w_gate, ffn1_w_up, ffn1_w_down, mix_norm, w_in, q_norm, k_norm, conv_w, conv_b, gate_a_w, gate_a_b, gate_x_w, gate_x_b, lru_lambda, w_out, ffn2_norm, ffn2_w_gate, ffn2_w_up, ffn2_w_down):
    batch, seq, _ = x.shape
    depth = w_in.shape[0]
    n = batch * seq

    wg1, wu1, wd1 = (w.astype(BF16) for w in (ffn1_w_gate, ffn1_w_up, ffn1_w_down))
    wg2, wu2, wd2 = (w.astype(BF16) for w in (ffn2_w_gate, ffn2_w_up, ffn2_w_down))
    w_in_b = w_in.astype(BF16)
    w_out_b = w_out.astype(BF16)
    w_gates = jnp.concatenate([_block_diag(gate_a_w), _block_diag(gate_x_w)],
                              axis=-1).astype(BF16)
    head_id = jnp.arange(MXU_DIM) // HEAD_DIM
    head_mean = ((head_id[:, None] == head_id[None, :]).astype(F32) / HEAD_DIM).astype(BF16)
    score_scale = 1.0 / math.sqrt(HEAD_DIM)
    q_gain = (jnp.tile(q_norm, (1, N_QKV_HEADS)) * score_scale).reshape(depth, 1, D_ATT_QKV)
    k_gain = jnp.tile(k_norm, (1, N_QKV_HEADS)).reshape(depth, 1, D_ATT_QKV)
    bias = _attention_bias()

    def row(v, l):
        return v[l].reshape(1, -1)

    xf = x.reshape(n, D_MODEL)
    for l in range(depth):
        xf = _ffn(xf, row(ffn1_norm, l), wg1[l], wu1[l], wd1[l])
        qkv0, qkv1, qkv2, y = _mixer_in(
            xf, row(mix_norm, l), w_in_b[l], head_mean, q_gain[l], k_gain[l],
            conv_w[l], row(conv_b, l), w_gates[l], row(gate_a_b, l), row(gate_x_b, l),
            row(lru_lambda, l), batch, seq)
        att = _attention((qkv0, qkv1, qkv2), bias, batch, seq)
        xf = _ffn(xf, row(ffn2_norm, l), wg2[l], wu2[l], wd2[l],
                  mixer_out=(att, y, w_out_b[l]))
    return xf.reshape(batch, seq, D_MODEL)
```

```python
import functools
import math

import jax
import jax.numpy as jnp
from jax import lax
from jax.experimental import pallas as pl
from jax.experimental.pallas import tpu as pltpu

F32 = jnp.float32
BF16 = jnp.bfloat16

D_MODEL = 1024
HEAD_DIM = 64
ATT_GROUPS = ((128, 1), (512, 4), (2048, 16))
N_GROUPS = len(ATT_GROUPS)
HEADS_PER_GROUP = 4
N_QKV_HEADS = N_GROUPS * HEADS_PER_GROUP
D_ATT_QKV = N_QKV_HEADS * HEAD_DIM
D_ATT_OUT = HEADS_PER_GROUP * HEAD_DIM
ALIBI_MAX_BIAS = 8.0
D_LRU = 768
LRU_BLOCK = 64
CONV_WIDTH = 4
LRU_C = 8.0
D_FF = 2816
FFN_RES_WEIGHT = 0.5
RMS_EPS = 1e-6
NEG_INF = -1e30

LANES = 128
SUBLANES = 8
MXU_DIM = 256
VMEM_LIMIT = 56 * 1024 * 1024

FF_CHUNK = MXU_DIM
N_FF_CHUNKS = D_FF // FF_CHUNK
TM_FFN = 512
TM_PROJ = 512
ATT_BLOCK = 128
HEAD_PAIR = LANES
N_PAIRS = D_ATT_OUT // HEAD_PAIR
MAX_DIL = ATT_GROUPS[-1][1]
ATT_UNROLL = 16
HEADS_PER_MXU = MXU_DIM // HEAD_DIM
LRU_GATE_BLK = MXU_DIM
N_LRU_LANE_TILES = D_LRU // LANES
LRU_SEG = TM_PROJ // SUBLANES
LRU_PITCH = LRU_SEG + SUBLANES


def _resident(block_shape, index_map):
    return pl.BlockSpec(block_shape, index_map, pipeline_mode=pl.Buffered(1))


def _rms_norm(x, gain):
    ms = jnp.mean(x * x, axis=-1, keepdims=True)
    return x * lax.rsqrt(ms + RMS_EPS) * gain


def _sigmoid(x):
    return 0.5 + 0.5 * jnp.tanh(0.5 * x)


def _ffn_kernel(*refs, with_mixer_out):
    if with_mixer_out:
        x_ref, att_ref, y_ref, wo_ref, gain_ref, wg_ref, wu_ref, wd_ref, o_ref, acc_ref = refs
        mix = jnp.concatenate(
            [att_ref[p].astype(BF16) for p in range(N_PAIRS)] + [y_ref[...]], axis=-1)
        x = x_ref[...] + jnp.dot(mix, wo_ref[...].astype(BF16),
                                 preferred_element_type=F32)
    else:
        x_ref, gain_ref, wg_ref, wu_ref, wd_ref, o_ref, acc_ref = refs
        x = x_ref[...]
    h = _rms_norm(x, gain_ref[...]).astype(BF16)
    for j in range(N_FF_CHUNKS):
        cols = slice(j * FF_CHUNK, (j + 1) * FF_CHUNK)
        g = jnp.dot(h, wg_ref[:, cols].astype(BF16), preferred_element_type=F32)
        u = jnp.dot(h, wu_ref[:, cols].astype(BF16), preferred_element_type=F32)
        a = (g / (1.0 + jnp.exp(-g)) * u).astype(BF16)
        d = jnp.dot(a, wd_ref[cols, :].astype(BF16), preferred_element_type=F32)
        if j == 0:
            acc_ref[...] = d
        else:
            acc_ref[...] += d
    o_ref[...] = x + FFN_RES_WEIGHT * acc_ref[...]


def _ffn(x, layer, gain, wg, wu, wd, mixer_out=None):
    n = x.shape[0]
    row_tile = pl.BlockSpec((TM_FFN, D_MODEL), lambda i: (i, 0))

    def layer_slice(rows, cols):
        return _resident((None, rows, cols), lambda i: (layer, 0, 0))

    in_specs, args = [row_tile], [x]
    if mixer_out is not None:
        att, y, w_out = mixer_out
        in_specs += [
            pl.BlockSpec((N_PAIRS, TM_FFN, HEAD_PAIR), lambda i: (0, i, 0)),
            pl.BlockSpec((TM_FFN, D_LRU), lambda i: (i, 0)),
            layer_slice(D_ATT_OUT + D_LRU, D_MODEL),
        ]
        args += [att, y, w_out]
    in_specs += [
        layer_slice(1, D_MODEL),
        layer_slice(D_MODEL, D_FF),
        layer_slice(D_MODEL, D_FF),
        layer_slice(D_FF, D_MODEL),
    ]
    args += [gain.reshape(gain.shape[0], 1, D_MODEL), wg, wu, wd]
    return pl.pallas_call(
        functools.partial(_ffn_kernel, with_mixer_out=mixer_out is not None),
        out_shape=jax.ShapeDtypeStruct((n, D_MODEL), F32),
        grid=(n // TM_FFN,),
        in_specs=in_specs,
        out_specs=row_tile,
        scratch_shapes=[pltpu.VMEM((TM_FFN, D_MODEL), F32)],
        compiler_params=pltpu.CompilerParams(
            dimension_semantics=("arbitrary",), vmem_limit_bytes=VMEM_LIMIT),
        name="ffn_mixer_out" if mixer_out is not None else "ffn",
    )(*args)


def _mixer_in_kernel(x_ref, gain_ref, w_ref, hm_ref, qg_ref, kg_ref,
                     cw_ref, cb_ref, wgate_ref, ba_ref, bx_ref, lam_ref,
                     o0_ref, o1_ref, o2_ref, y_ref,
                     slab_ref, xs_ref, a_ref, b_ref, h_ref):
    @pl.when(pl.program_id(1) == 0)
    def _():
        xs_ref[0:SUBLANES, :] = jnp.zeros((SUBLANES, D_LRU), F32)
        h_ref[...] = jnp.zeros(h_ref.shape, F32)

    x = x_ref[...]
    h = _rms_norm(x, gain_ref[...]).astype(BF16)

    def head_norm(t, gain):
        sq = (t * t).astype(BF16)
        ms = jnp.concatenate(
            [jnp.dot(sq[:, i * MXU_DIM:(i + 1) * MXU_DIM], hm_ref[...],
                     preferred_element_type=F32)
             for i in range(D_ATT_QKV // MXU_DIM)], axis=-1)
        return t * lax.rsqrt(ms + RMS_EPS) * gain

    def zero_after(val):
        tail = val[val.shape[0] - SUBLANES:, :]
        bits = None
        for j in range(val.shape[1] // LANES):
            piece = lax.bitcast_convert_type(tail[:, j * LANES:(j + 1) * LANES], jnp.uint32)
            bits = piece if bits is None else bits | piece
        bits = lax.shift_right_logical(lax.shift_right_logical(bits, jnp.uint32(16)),
                                       jnp.uint32(16))
        return bits[0:1, :]

    def ordered_after(row, zero_bits):
        tiled = jnp.concatenate([zero_bits] * (row.shape[1] // LANES), axis=-1)
        return lax.bitcast_convert_type(
            lax.bitcast_convert_type(row, jnp.uint32) | tiled, F32)

    c0 = 3 * D_ATT_QKV
    lru_x = jnp.dot(h, w_ref[:, c0:c0 + D_LRU], preferred_element_type=F32)
    gate = jnp.dot(h, w_ref[:, c0 + D_LRU:c0 + 2 * D_LRU], preferred_element_type=F32)
    q = jnp.dot(h, w_ref[:, 0:D_ATT_QKV], preferred_element_type=F32)

    xs_ref[SUBLANES:, :] = lru_x
    xc = cb_ref[...]
    for w in range(CONV_WIDTH):
        start = SUBLANES - (CONV_WIDTH - 1) + w
        xc = xc + cw_ref[w:w + 1, :] * xs_ref[pl.ds(start, TM_PROJ), :]
    xs_ref[0:SUBLANES, :] = lru_x[TM_PROJ - SUBLANES:, :]

    xcb = xc.astype(BF16)
    gates = [jnp.dot(xcb[:, i * LRU_GATE_BLK:(i + 1) * LRU_GATE_BLK], wgate_ref[i],
                     preferred_element_type=F32)
             for i in range(D_LRU // LRU_GATE_BLK)]
    k = jnp.dot(h, w_ref[:, D_ATT_QKV:2 * D_ATT_QKV], preferred_element_type=F32)
    v = jnp.dot(h, w_ref[:, 2 * D_ATT_QKV:3 * D_ATT_QKV], preferred_element_type=F32)
    bias_a = ordered_after(ba_ref[...], zero_after(q))
    ga = jnp.concatenate([t[:, :LRU_GATE_BLK] for t in gates], axis=-1) + bias_a
    gx = jnp.concatenate([t[:, LRU_GATE_BLK:] for t in gates], axis=-1) + bx_ref[...]
    z = -lam_ref[...]
    softplus = jnp.maximum(z, 0.0) + jnp.log1p(jnp.exp(-jnp.abs(z)))
    log_a = (-LRU_C * softplus) * _sigmoid(ga)
    a = jnp.exp(log_a)
    b = jnp.sqrt(1.0 - a * a) * (_sigmoid(gx) * xc)

    zero_k = zero_after(k)
    zero_v = zero_after(v)
    h_cols = []
    for lt in range(N_LRU_LANE_TILES):
        cols = slice(lt * LANES, (lt + 1) * LANES)
        for s in range(SUBLANES):
            rows = slice(s * LRU_SEG, (s + 1) * LRU_SEG)
            a_ref[lt, pl.ds(s * LRU_PITCH, LRU_SEG), :] = a[rows, cols]
            b_ref[lt, pl.ds(s * LRU_PITCH, LRU_SEG), :] = b[rows, cols]
        h_loc = jnp.zeros((SUBLANES, LANES), F32)
        a_cum = jnp.ones((SUBLANES, LANES), F32)
        for i in range(LRU_SEG):
            idx = pl.ds(i, SUBLANES, stride=LRU_PITCH)
            a_i = a_ref[lt, idx, :]
            h_loc = a_i * h_loc + b_ref[lt, idx, :]
            a_cum = a_cum * a_i
            b_ref[lt, idx, :] = h_loc
            a_ref[lt, idx, :] = a_cum
        carry = ordered_after(h_ref[lt, 0:1, :],
                              zero_k if lt < N_LRU_LANE_TILES // 2 else zero_v)
        segs = []
        for s in range(SUBLANES):
            rows = pl.ds(s * LRU_PITCH, LRU_SEG)
            segs.append(b_ref[lt, rows, :] + a_ref[lt, rows, :] * carry)
            carry = a_cum[s:s + 1, :] * carry + h_loc[s:s + 1, :]
        h_ref[lt, 0:1, :] = carry
        h_cols.append(jnp.concatenate(segs, axis=0))
    h_all = jnp.concatenate(h_cols, axis=-1)

    inner = math.sqrt(2.0 / math.pi) * (gate + 0.044715 * (gate * gate * gate))
    gelu = 0.5 * gate * (1.0 + jnp.tanh(inner))
    y_ref[...] = (h_all * gelu).astype(y_ref.dtype)

    qn = head_norm(q, qg_ref[...])
    kn = head_norm(k, kg_ref[...])
    outs = (o0_ref, o1_ref, o2_ref)
    slab = 0
    for g, (window, dil) in enumerate(ATT_GROUPS):
        for part, val in enumerate((qn, kn, v)):
            for p in range(N_PAIRS):
                c = g * D_ATT_OUT + p * HEAD_PAIR
                piece = val[:, c:c + HEAD_PAIR]
                dst = outs[g].at[part * N_PAIRS + p]
                if dil == 1:
                    dst[0] = piece.astype(BF16)
                    continue
                slab_ref[slab] = piece
                for cls in range(dil):
                    rows = pl.ds(cls, TM_PROJ // dil, stride=dil)
                    dst[cls] = slab_ref[slab, rows, :].astype(BF16)
                slab += 1


def _mixer_in(x, layer, params, batch, seq):
    n = x.shape[0]
    tiles_per_seq = seq // TM_PROJ

    def token_tile(width):
        return pl.BlockSpec((TM_PROJ, width), lambda b, t: (b * tiles_per_seq + t, 0))

    def param_spec(p):
        if p.ndim == 2:
            return _resident(p.shape, lambda b, t: (0, 0))
        zeros = (0,) * (p.ndim - 1)
        return _resident((None, *p.shape[1:]), lambda b, t: (layer, *zeros))

    qkv_shapes, qkv_specs = [], []
    for window, dil in ATT_GROUPS:
        qkv_shapes.append(jax.ShapeDtypeStruct(
            (3 * N_PAIRS, batch, dil, seq // dil, HEAD_PAIR), BF16))
        qkv_specs.append(pl.BlockSpec(
            (3 * N_PAIRS, None, dil, TM_PROJ // dil, HEAD_PAIR),
            lambda b, t: (0, b, 0, t, 0)))
    n_slabs = 3 * N_PAIRS * sum(1 for _, dil in ATT_GROUPS if dil > 1)
    scan_slab = pltpu.VMEM((N_LRU_LANE_TILES, SUBLANES * LRU_PITCH, LANES), F32)
    return pl.pallas_call(
        _mixer_in_kernel,
        out_shape=(*qkv_shapes, jax.ShapeDtypeStruct((n, D_LRU), BF16)),
        grid=(batch, tiles_per_seq),
        in_specs=[token_tile(D_MODEL)] + [param_spec(p) for p in params],
        out_specs=(*qkv_specs, token_tile(D_LRU)),
        scratch_shapes=[
            pltpu.VMEM((n_slabs, TM_PROJ, HEAD_PAIR), F32),
            pltpu.VMEM((TM_PROJ + SUBLANES, D_LRU), F32),
            scan_slab, scan_slab,
            pltpu.VMEM((N_LRU_LANE_TILES, SUBLANES, LANES), F32),
        ],
        compiler_params=pltpu.CompilerParams(
            dimension_semantics=("arbitrary", "arbitrary"), vmem_limit_bytes=VMEM_LIMIT),
        name="mixer_in",
    )(x, *params)


def _attn_kernel(q0_ref, k0_ref, v0_ref, q1_ref, k1_ref, v1_ref,
                 q2_ref, k2_ref, v2_ref, bias_ref, out_ref, so_ref, sl_ref):
    qkv = ((q0_ref, k0_ref, v0_ref), (q1_ref, k1_ref, v1_ref), (q2_ref, k2_ref, v2_ref))
    seq = out_ref.shape[0]
    n_blocks = seq // ATT_BLOCK
    lane = lax.broadcasted_iota(jnp.int32, (ATT_BLOCK, HEAD_PAIR), 1)
    head0 = lane < HEAD_DIM

    def block(g, idx, blocks_per_class):
        q_ref, k_ref, v_ref = qkv[g]
        first = (idx & (blocks_per_class - 1)) == 0
        q_start = pl.multiple_of(idx * ATT_BLOCK, ATT_BLOCK)
        k_start = pl.multiple_of(jnp.where(first, idx, idx - 1) * ATT_BLOCK, ATT_BLOCK)
        table = jnp.where(first, 1, 0)
        q = q_ref[pl.ds(q_start, ATT_BLOCK), :]
        k = k_ref[pl.ds(k_start, 2 * ATT_BLOCK), :]
        v = v_ref[pl.ds(k_start, 2 * ATT_BLOCK), :]
        zero = jnp.zeros_like(q)
        q2 = jnp.concatenate([jnp.where(head0, q, zero), jnp.where(head0, zero, q)], axis=0)
        s2 = lax.dot_general(q2, k, (((1,), (1,)), ((), ())), preferred_element_type=F32)
        es, ms, ls = [], [], []
        for hh in range(2):
            s = s2[hh * ATT_BLOCK:(hh + 1) * ATT_BLOCK] + bias_ref[g, hh, table]
            m = jnp.max(s, axis=-1, keepdims=True)
            e = jnp.exp(s - m)
            ls.append(jnp.sum(e, axis=-1, keepdims=True))
            ms.append(m)
            es.append(e.astype(BF16))
        o2 = jnp.dot(jnp.concatenate(es, axis=0), v, preferred_element_type=F32)
        l_pair = jnp.where(head0, ls[0], ls[1])
        m_pair = jnp.where(head0, ms[0], ms[1])
        o_pair = jnp.where(head0, o2[:ATT_BLOCK], o2[ATT_BLOCK:])
        so_ref[g, pl.ds(q_start, ATT_BLOCK), :] = o_pair / l_pair
        sl_ref[g, pl.ds(q_start, ATT_BLOCK), :] = m_pair + jnp.log(l_pair)

    for g, (window, dil) in enumerate(ATT_GROUPS):
        blocks_per_class = n_blocks // dil

        def body(it, carry, g=g, blocks_per_class=blocks_per_class):
            for u in range(ATT_UNROLL):
                block(g, it * ATT_UNROLL + u, blocks_per_class)
            return carry

        lax.fori_loop(0, n_blocks // ATT_UNROLL, body, 0)

    rows = seq // MAX_DIL
    for c16 in range(MAX_DIL):
        os_, ls_ = [], []
        for g, (window, dil) in enumerate(ATT_GROUPS):
            stride = MAX_DIL // dil
            start = (c16 % dil) * (seq // dil) + c16 // dil
            idx = pl.ds(start, rows) if stride == 1 else pl.ds(start, rows, stride=stride)
            os_.append(so_ref[g, idx, :])
            ls_.append(sl_ref[g, idx, :])
        m = jnp.maximum(jnp.maximum(ls_[0], ls_[1]), ls_[2])
        es = [jnp.exp(l - m) for l in ls_]
        num = es[0] * os_[0] + es[1] * os_[1] + es[2] * os_[2]
        out_ref[pl.ds(c16, rows, stride=MAX_DIL), :] = num / (es[0] + es[1] + es[2])


def _attention(qkv_groups, bias, batch, seq):
    in_specs, args = [], []
    for g, (window, dil) in enumerate(ATT_GROUPS):
        arr = qkv_groups[g].reshape(3 * N_PAIRS, batch, seq, HEAD_PAIR)
        for part in range(3):
            in_specs.append(pl.BlockSpec(
                (None, None, seq, HEAD_PAIR),
                lambda b, p, part=part: (part * N_PAIRS + p, b, 0, 0)))
            args.append(arr)
    in_specs.append(pl.BlockSpec(
        (N_GROUPS, 2, 2, ATT_BLOCK, 2 * ATT_BLOCK), lambda b, p: (0, p, 0, 0, 0)))
    args.append(bias)
    scratch = pltpu.VMEM((N_GROUPS, seq, HEAD_PAIR), F32)
    out = pl.pallas_call(
        _attn_kernel,
        out_shape=jax.ShapeDtypeStruct((N_PAIRS, batch, seq, HEAD_PAIR), F32),
        grid=(batch, N_PAIRS),
        in_specs=in_specs,
        out_specs=pl.BlockSpec((None, None, seq, HEAD_PAIR), lambda b, p: (p, b, 0, 0)),
        scratch_shapes=[scratch, scratch],
        compiler_params=pltpu.CompilerParams(
            dimension_semantics=("arbitrary", "arbitrary"), vmem_limit_bytes=VMEM_LIMIT),
        name="dilated_attention",
    )(*args)
    return out.reshape(N_PAIRS, batch * seq, HEAD_PAIR)


def _attention_bias():
    n = ATT_BLOCK
    heads = jnp.arange(1, N_QKV_HEADS + 1, dtype=F32)
    slopes = jnp.exp2(-ALIBI_MAX_BIAS * heads / N_QKV_HEADS).reshape(N_GROUPS, HEADS_PER_GROUP)
    qi = jnp.arange(n)[:, None]
    ki = jnp.arange(2 * n)[None, :]
    tables = []
    for g, (window, dil) in enumerate(ATT_GROUPS):
        steps = n + qi - ki
        valid = (steps >= 0) & (steps <= n)
        steps_f = qi - ki
        valid_f = steps_f >= 0
        per_head = []
        for hh in range(HEADS_PER_GROUP):
            sl = slopes[g, hh]
            reg = jnp.where(valid, -sl * (dil * steps).astype(F32), NEG_INF)
            fst = jnp.where(valid_f, -sl * (dil * steps_f).astype(F32), NEG_INF)
            per_head.append(jnp.stack([reg, fst]))
        tables.append(jnp.stack(per_head))
    return jnp.stack(tables).astype(F32)


def _block_diag(w):
    depth = w.shape[0]
    per = LRU_GATE_BLK // LRU_BLOCK
    w = w.reshape(depth, D_LRU // LRU_GATE_BLK, per, LRU_BLOCK, LRU_BLOCK)
    eye = jnp.eye(per, dtype=w.dtype)
    full = w[:, :, :, :, None, :] * eye[None, None, :, None, :, None]
    return full.reshape(depth, D_LRU // LRU_GATE_BLK, LRU_GATE_BLK, LRU_GATE_BLK)


def kernel(x, ffn1_norm, ffn1_w_gate, ffn1_w_up, ffn1_w_down, mix_norm, w_in, q_norm, k_norm, conv_w, conv_b, gate_a_w, gate_a_b, gate_x_w, gate_x_b, lru_lambda, w_out, ffn2_norm, ffn2_w_gate, ffn2_w_up, ffn2_w_down):
    batch, seq, _ = x.shape
    depth = w_in.shape[0]
    n = batch * seq

    w_in_b = w_in.astype(BF16)
    w_gates = jnp.concatenate([_block_diag(gate_a_w), _block_diag(gate_x_w)],
                              axis=-1).astype(BF16)
    head_id = jnp.arange(MXU_DIM) // HEAD_DIM
    head_mean = ((head_id[:, None] == head_id[None, :]).astype(F32) / HEAD_DIM).astype(BF16)
    score_scale = 1.0 / math.sqrt(HEAD_DIM)
    q_gain = (jnp.tile(q_norm, (1, N_QKV_HEADS)) * score_scale).reshape(depth, 1, D_ATT_QKV)
    k_gain = jnp.tile(k_norm, (1, N_QKV_HEADS)).reshape(depth, 1, D_ATT_QKV)
    bias = _attention_bias()

    def rows(v):
        return v.reshape(depth, 1, v.shape[-1])

    mixer_params = (rows(mix_norm), w_in_b, head_mean, q_gain, k_gain, conv_w, rows(conv_b),
                    w_gates, rows(gate_a_b), rows(gate_x_b), rows(lru_lambda))
    xf = x.reshape(n, D_MODEL)
    for l in range(depth):
        xf = _ffn(xf, l, ffn1_norm, ffn1_w_gate, ffn1_w_up, ffn1_w_down)
        qkv0, qkv1, qkv2, y = _mixer_in(xf, l, mixer_params, batch, seq)
        att = _attention((qkv0, qkv1, qkv2), bias, batch, seq)
        xf = _ffn(xf, l, ffn2_norm, ffn2_w_gate, ffn2_w_up, ffn2_w_down,
                  mixer_out=(att, y, w_out))
    return xf.reshape(batch, seq, D_MODEL)
```

```python
import functools
import math

import jax
import jax.numpy as jnp
from jax import lax
from jax.experimental import pallas as pl
from jax.experimental.pallas import tpu as pltpu

F32 = jnp.float32
BF16 = jnp.bfloat16

D_MODEL = 1024
HEAD_DIM = 64
ATT_GROUPS = ((128, 1), (512, 4), (2048, 16))
N_GROUPS = len(ATT_GROUPS)
HEADS_PER_GROUP = 4
N_QKV_HEADS = N_GROUPS * HEADS_PER_GROUP
D_ATT_QKV = N_QKV_HEADS * HEAD_DIM
D_ATT_OUT = HEADS_PER_GROUP * HEAD_DIM
ALIBI_MAX_BIAS = 8.0
D_LRU = 768
LRU_BLOCK = 64
CONV_WIDTH = 4
LRU_C = 8.0
D_FF = 2816
FFN_RES_WEIGHT = 0.5
RMS_EPS = 1e-6
NEG_INF = -1e30

LANES = 128
SUBLANES = 8
MXU_DIM = 256
VMEM_LIMIT = 56 * 1024 * 1024

FF_CHUNK = MXU_DIM
N_FF_CHUNKS = D_FF // FF_CHUNK
TM_FFN = 512
TM_PROJ = 512
ATT_BLOCK = 128
HEAD_PAIR = LANES
N_PAIRS = D_ATT_OUT // HEAD_PAIR
MAX_DIL = ATT_GROUPS[-1][1]
ATT_UNROLL = 16
HEADS_PER_MXU = MXU_DIM // HEAD_DIM
LRU_GATE_BLK = MXU_DIM
N_LRU_LANE_TILES = D_LRU // LANES
LRU_SEG = TM_PROJ // SUBLANES
LRU_PITCH = LRU_SEG + SUBLANES
CLS16_PITCH = TM_PROJ // MAX_DIL + SUBLANES


def _resident(block_shape, index_map):
    return pl.BlockSpec(block_shape, index_map, pipeline_mode=pl.Buffered(1))


def _rms_norm(x, gain):
    ms = jnp.mean(x * x, axis=-1, keepdims=True)
    return x * lax.rsqrt(ms + RMS_EPS) * gain


def _ffn_kernel(*refs, with_mixer_out):
    if with_mixer_out:
        x_ref, att_ref, y_ref, wo_ref, gain_ref, wg_ref, wu_ref, wd_ref, o_ref, acc_ref = refs
        mix = jnp.concatenate(
            [att_ref[p].astype(BF16) for p in range(N_PAIRS)] + [y_ref[...]], axis=-1)
        x = x_ref[...] + jnp.dot(mix, wo_ref[...].astype(BF16),
                                 preferred_element_type=F32)
    else:
        x_ref, gain_ref, wg_ref, wu_ref, wd_ref, o_ref, acc_ref = refs
        x = x_ref[...]
    h = _rms_norm(x, gain_ref[...]).astype(BF16)
    for j in range(N_FF_CHUNKS):
        cols = slice(j * FF_CHUNK, (j + 1) * FF_CHUNK)
        g = jnp.dot(h, wg_ref[:, cols].astype(BF16), preferred_element_type=F32)
        u = jnp.dot(h, wu_ref[:, cols].astype(BF16), preferred_element_type=F32)
        a = (g / (1.0 + jnp.exp(-g)) * u).astype(BF16)
        d = jnp.dot(a, wd_ref[cols, :].astype(BF16), preferred_element_type=F32)
        if j == 0:
            acc_ref[...] = d
        else:
            acc_ref[...] += d
    o_ref[...] = x + FFN_RES_WEIGHT * acc_ref[...]


def _ffn(x, layer, gain, wg, wu, wd, mixer_out=None):
    n = x.shape[0]
    row_tile = pl.BlockSpec((TM_FFN, D_MODEL), lambda i: (i, 0))

    def layer_slice(rows, cols):
        return _resident((None, rows, cols), lambda i: (layer, 0, 0))

    in_specs, args = [row_tile], [x]
    if mixer_out is not None:
        att, y, w_out = mixer_out
        in_specs += [
            pl.BlockSpec((N_PAIRS, TM_FFN, HEAD_PAIR), lambda i: (0, i, 0)),
            pl.BlockSpec((TM_FFN, D_LRU), lambda i: (i, 0)),
            layer_slice(D_ATT_OUT + D_LRU, D_MODEL),
        ]
        args += [att, y, w_out]
    in_specs += [
        layer_slice(1, D_MODEL),
        layer_slice(D_MODEL, D_FF),
        layer_slice(D_MODEL, D_FF),
        layer_slice(D_FF, D_MODEL),
    ]
    args += [gain.reshape(gain.shape[0], 1, D_MODEL), wg, wu, wd]
    return pl.pallas_call(
        functools.partial(_ffn_kernel, with_mixer_out=mixer_out is not None),
        out_shape=jax.ShapeDtypeStruct((n, D_MODEL), F32),
        grid=(n // TM_FFN,),
        in_specs=in_specs,
        out_specs=row_tile,
        scratch_shapes=[pltpu.VMEM((TM_FFN, D_MODEL), F32)],
        compiler_params=pltpu.CompilerParams(
            dimension_semantics=("arbitrary",), vmem_limit_bytes=VMEM_LIMIT),
        name="ffn_mixer_out" if mixer_out is not None else "ffn",
    )(*args)


def _mixer_in_kernel(x_ref, gain_ref, w_ref, hm_ref, qg_ref, kg_ref,
                     cw_ref, cb_ref, wgate_ref, ba_ref, bx_ref, lam_ref,
                     o0_ref, o1_ref, o2_ref, y_ref,
                     slab_ref, slab16_ref, seg_ref, hist_ref, h_ref):
    @pl.when(pl.program_id(1) == 0)
    def _():
        hist_ref[...] = jnp.zeros(hist_ref.shape, F32)
        h_ref[...] = jnp.zeros(h_ref.shape, F32)

    x = x_ref[...]
    h = _rms_norm(x, gain_ref[...]).astype(BF16)

    def head_norm(t, gain):
        sq = (t * t).astype(BF16)
        ms = jnp.concatenate(
            [jnp.dot(sq[:, i * MXU_DIM:(i + 1) * MXU_DIM], hm_ref[...],
                     preferred_element_type=F32)
             for i in range(D_ATT_QKV // MXU_DIM)], axis=-1)
        return t * lax.rsqrt(ms + RMS_EPS) * gain

    def zero_after(val):
        tail = val[val.shape[0] - SUBLANES:, :]
        bits = None
        for j in range(val.shape[1] // LANES):
            piece = lax.bitcast_convert_type(tail[:, j * LANES:(j + 1) * LANES], jnp.uint32)
            bits = piece if bits is None else bits | piece
        bits = lax.shift_right_logical(lax.shift_right_logical(bits, jnp.uint32(16)),
                                       jnp.uint32(16))
        return bits[0:1, :]

    def ordered_after(row, zero_bits):
        tiled = jnp.concatenate([zero_bits] * (row.shape[1] // LANES), axis=-1)
        return lax.bitcast_convert_type(
            lax.bitcast_convert_type(row, jnp.uint32) | tiled, F32)

    c0 = 3 * D_ATT_QKV
    lru_x = jnp.dot(h, w_ref[:, c0:c0 + D_LRU], preferred_element_type=F32)
    gate = jnp.dot(h, w_ref[:, c0 + D_LRU:c0 + 2 * D_LRU], preferred_element_type=F32)
    q = jnp.dot(h, w_ref[:, 0:D_ATT_QKV], preferred_element_type=F32)

    sub = lax.broadcasted_iota(jnp.int32, (SUBLANES, LANES), 0)
    seg_rows = [slice(i * SUBLANES, (i + 1) * SUBLANES) for i in range(LRU_SEG)]

    def to_segment_order(col, lt):
        for s in range(SUBLANES):
            seg_ref[lt, pl.ds(s * LRU_PITCH, LRU_SEG), :] = col[s * LRU_SEG:(s + 1) * LRU_SEG]
        return [seg_ref[lt, pl.ds(i, SUBLANES, stride=LRU_PITCH), :] for i in range(LRU_SEG)]

    def to_time_order(vregs, lt):
        for i, vreg in enumerate(vregs):
            seg_ref[lt, pl.ds(i, SUBLANES, stride=LRU_PITCH), :] = vreg
        return jnp.concatenate(
            [seg_ref[lt, pl.ds(s * LRU_PITCH, LRU_SEG), :] for s in range(SUBLANES)], axis=0)

    n_hist = CONV_WIDTH - 1
    xc_cols = []
    for lt in range(N_LRU_LANE_TILES):
        lanes = slice(lt * LANES, (lt + 1) * LANES)
        xv = to_segment_order(lru_x[:, lanes], lt)
        head = []
        for j in range(n_hist):
            last = xv[LRU_SEG - n_hist + j]
            prev = hist_ref[lt, seg_rows[j], :]
            head.append(jnp.where(sub == 0, pltpu.roll(prev, 1, 0), pltpu.roll(last, 1, 0)))
            hist_ref[lt, seg_rows[j], :] = last
        ext = jnp.concatenate(head + xv, axis=0)
        xc_l = cb_ref[:, lanes]
        for w in range(CONV_WIDTH):
            xc_l = xc_l + cw_ref[w:w + 1, lanes] * ext[w * SUBLANES:w * SUBLANES + TM_PROJ]
        xc_cols.append(xc_l)
    xc = jnp.concatenate(xc_cols, axis=-1)

    xcb = xc.astype(BF16)
    gates = [jnp.dot(xcb[:, i * LRU_GATE_BLK:(i + 1) * LRU_GATE_BLK], wgate_ref[i],
                     preferred_element_type=F32)
             for i in range(D_LRU // LRU_GATE_BLK)]
    k = jnp.dot(h, w_ref[:, D_ATT_QKV:2 * D_ATT_QKV], preferred_element_type=F32)
    v = jnp.dot(h, w_ref[:, 2 * D_ATT_QKV:3 * D_ATT_QKV], preferred_element_type=F32)
    bias_a = ordered_after(ba_ref[...], zero_after(q))
    tanh_a = jnp.tanh(jnp.concatenate([t[:, :LRU_GATE_BLK] for t in gates], axis=-1) + bias_a)
    tanh_x = jnp.tanh(jnp.concatenate([t[:, LRU_GATE_BLK:] for t in gates], axis=-1)
                      + bx_ref[...])
    z = -lam_ref[...]
    softplus = jnp.maximum(z, 0.0) + jnp.log1p(jnp.exp(-jnp.abs(z)))
    half_decay = (-0.5 * LRU_C) * softplus
    a = jnp.exp(half_decay + half_decay * tanh_a)
    half_xc = 0.5 * xc
    b = jnp.sqrt(1.0 - a * a) * (half_xc + half_xc * tanh_x)

    zero_k = zero_after(k)
    zero_v = zero_after(v)
    h_cols = []
    for lt in range(N_LRU_LANE_TILES):
        lanes = slice(lt * LANES, (lt + 1) * LANES)
        a_l, b_l = a[:, lanes], b[:, lanes]
        h_loc = jnp.zeros((SUBLANES, LANES), F32)
        a_cum = jnp.ones((SUBLANES, LANES), F32)
        h_locs, a_cums = [], []
        for rows in seg_rows:
            h_loc = a_l[rows] * h_loc + b_l[rows]
            a_cum = a_cum * a_l[rows]
            h_locs.append(h_loc)
            a_cums.append(a_cum)
        carry = ordered_after(h_ref[lt, 0:1, :],
                              zero_k if lt < N_LRU_LANE_TILES // 2 else zero_v)
        carry_in = jnp.zeros((SUBLANES, LANES), F32)
        for s in range(SUBLANES):
            carry_in = jnp.where(sub == s, carry, carry_in)
            carry = a_cum[s:s + 1, :] * carry + h_loc[s:s + 1, :]
        h_ref[lt, 0:1, :] = carry
        h_cols.append(to_time_order(
            [hl + ac * carry_in for hl, ac in zip(h_locs, a_cums)], lt))
    h_all = jnp.concatenate(h_cols, axis=-1)

    inner = math.sqrt(2.0 / math.pi) * (gate + 0.044715 * (gate * gate * gate))
    gelu = 0.5 * gate * (1.0 + jnp.tanh(inner))
    y_ref[...] = (h_all * gelu).astype(y_ref.dtype)

    qn = head_norm(q, qg_ref[...])
    kn = head_norm(k, kg_ref[...])
    outs = (o0_ref, o1_ref, o2_ref)
    for g, (window, dil) in enumerate(ATT_GROUPS):
        for part, val in enumerate((qn, kn, v)):
            for p in range(N_PAIRS):
                c = g * D_ATT_OUT + p * HEAD_PAIR
                piece = val[:, c:c + HEAD_PAIR]
                dst = outs[g].at[part * N_PAIRS + p]
                slab = part * N_PAIRS + p
                if dil == 1:
                    dst[0] = piece.astype(BF16)
                elif dil == SUBLANES * 2:
                    per_class = TM_PROJ // dil
                    for i in range(TM_PROJ // SUBLANES):
                        start = (i % 2) * SUBLANES * CLS16_PITCH + i // 2
                        slab16_ref[slab, pl.ds(start, SUBLANES, stride=CLS16_PITCH), :] = (
                            piece[i * SUBLANES:(i + 1) * SUBLANES])
                    for cls in range(dil):
                        dst[cls] = slab16_ref[
                            slab, pl.ds(cls * CLS16_PITCH, per_class), :].astype(BF16)
                else:
                    slab_ref[slab] = piece
                    for cls in range(dil):
                        rows = pl.ds(cls, TM_PROJ // dil, stride=dil)
                        dst[cls] = slab_ref[slab, rows, :].astype(BF16)


def _mixer_in(x, layer, params, batch, seq):
    n = x.shape[0]
    tiles_per_seq = seq // TM_PROJ

    def token_tile(width):
        return pl.BlockSpec((TM_PROJ, width), lambda b, t: (b * tiles_per_seq + t, 0))

    def param_spec(p):
        if p.ndim == 2:
            return _resident(p.shape, lambda b, t: (0, 0))
        zeros = (0,) * (p.ndim - 1)
        return _resident((None, *p.shape[1:]), lambda b, t: (layer, *zeros))

    qkv_shapes, qkv_specs = [], []
    for window, dil in ATT_GROUPS:
        qkv_shapes.append(jax.ShapeDtypeStruct(
            (3 * N_PAIRS, batch, dil, seq // dil, HEAD_PAIR), BF16))
        qkv_specs.append(pl.BlockSpec(
            (3 * N_PAIRS, None, dil, TM_PROJ // dil, HEAD_PAIR),
            lambda b, t: (0, b, 0, t, 0)))
    return pl.pallas_call(
        _mixer_in_kernel,
        out_shape=(*qkv_shapes, jax.ShapeDtypeStruct((n, D_LRU), BF16)),
        grid=(batch, tiles_per_seq),
        in_specs=[token_tile(D_MODEL)] + [param_spec(p) for p in params],
        out_specs=(*qkv_specs, token_tile(D_LRU)),
        scratch_shapes=[
            pltpu.VMEM((3 * N_PAIRS, TM_PROJ, HEAD_PAIR), F32),
            pltpu.VMEM((3 * N_PAIRS, MAX_DIL * CLS16_PITCH, HEAD_PAIR), F32),
            pltpu.VMEM((N_LRU_LANE_TILES, SUBLANES * LRU_PITCH, LANES), F32),
            pltpu.VMEM((N_LRU_LANE_TILES, (CONV_WIDTH - 1) * SUBLANES, LANES), F32),
            pltpu.VMEM((N_LRU_LANE_TILES, SUBLANES, LANES), F32),
        ],
        compiler_params=pltpu.CompilerParams(
            dimension_semantics=("arbitrary", "arbitrary"), vmem_limit_bytes=VMEM_LIMIT),
        name="mixer_in",
    )(x, *params)


def _attn_kernel(q0_ref, k0_ref, v0_ref, q1_ref, k1_ref, v1_ref,
                 q2_ref, k2_ref, v2_ref, bias_ref, out_ref, so_ref, sl_ref):
    qkv = ((q0_ref, k0_ref, v0_ref), (q1_ref, k1_ref, v1_ref), (q2_ref, k2_ref, v2_ref))
    seq = out_ref.shape[0]
    n_blocks = seq // ATT_BLOCK
    lane = lax.broadcasted_iota(jnp.int32, (ATT_BLOCK, HEAD_PAIR), 1)
    head0 = lane < HEAD_DIM

    def block(g, idx, blocks_per_class):
        q_ref, k_ref, v_ref = qkv[g]
        first = (idx & (blocks_per_class - 1)) == 0
        q_start = pl.multiple_of(idx * ATT_BLOCK, ATT_BLOCK)
        k_start = pl.multiple_of(jnp.where(first, idx, idx - 1) * ATT_BLOCK, ATT_BLOCK)
        table = jnp.where(first, 1, 0)
        q = q_ref[pl.ds(q_start, ATT_BLOCK), :]
        k = k_ref[pl.ds(k_start, 2 * ATT_BLOCK), :]
        v = v_ref[pl.ds(k_start, 2 * ATT_BLOCK), :]
        zero = jnp.zeros_like(q)
        q2 = jnp.concatenate([jnp.where(head0, q, zero), jnp.where(head0, zero, q)], axis=0)
        s2 = lax.dot_general(q2, k, (((1,), (1,)), ((), ())), preferred_element_type=F32)
        es, ms, ls = [], [], []
        for hh in range(2):
            s = s2[hh * ATT_BLOCK:(hh + 1) * ATT_BLOCK] + bias_ref[g, hh, table]
            m = jnp.max(s, axis=-1, keepdims=True)
            e = jnp.exp(s - m)
            ls.append(jnp.sum(e, axis=-1, keepdims=True))
            ms.append(m)
            es.append(e.astype(BF16))
        o2 = jnp.dot(jnp.concatenate(es, axis=0), v, preferred_element_type=F32)
        l_pair = jnp.where(head0, ls[0], ls[1])
        m_pair = jnp.where(head0, ms[0], ms[1])
        o_pair = jnp.where(head0, o2[:ATT_BLOCK], o2[ATT_BLOCK:])
        so_ref[g, pl.ds(q_start, ATT_BLOCK), :] = o_pair / l_pair
        sl_ref[g, pl.ds(q_start, ATT_BLOCK), :] = m_pair + jnp.log(l_pair)

    for g, (window, dil) in enumerate(ATT_GROUPS):
        blocks_per_class = n_blocks // dil

        def body(it, carry, g=g, blocks_per_class=blocks_per_class):
            for u in range(ATT_UNROLL):
                block(g, it * ATT_UNROLL + u, blocks_per_class)
            return carry

        lax.fori_loop(0, n_blocks // ATT_UNROLL, body, 0)

    rows = seq // MAX_DIL
    for c16 in range(MAX_DIL):
        os_, ls_ = [], []
        for g, (window, dil) in enumerate(ATT_GROUPS):
            stride = MAX_DIL // dil
            start = (c16 % dil) * (seq // dil) + c16 // dil
            idx = pl.ds(start, rows) if stride == 1 else pl.ds(start, rows, stride=stride)
            os_.append(so_ref[g, idx, :])
            ls_.append(sl_ref[g, idx, :])
        m = jnp.maximum(jnp.maximum(ls_[0], ls_[1]), ls_[2])
        es = [jnp.exp(l - m) for l in ls_]
        num = es[0] * os_[0] + es[1] * os_[1] + es[2] * os_[2]
        out_ref[pl.ds(c16, rows, stride=MAX_DIL), :] = num / (es[0] + es[1] + es[2])


def _attention(qkv_groups, bias, batch, seq):
    in_specs, args = [], []
    for g, (window, dil) in enumerate(ATT_GROUPS):
        arr = qkv_groups[g].reshape(3 * N_PAIRS, batch, seq, HEAD_PAIR)
        for part in range(3):
            in_specs.append(pl.BlockSpec(
                (None, None, seq, HEAD_PAIR),
                lambda b, p, part=part: (part * N_PAIRS + p, b, 0, 0)))
            args.append(arr)
    in_specs.append(pl.BlockSpec(
        (N_GROUPS, 2, 2, ATT_BLOCK, 2 * ATT_BLOCK), lambda b, p: (0, p, 0, 0, 0)))
    args.append(bias)
    scratch = pltpu.VMEM((N_GROUPS, seq, HEAD_PAIR), F32)
    out = pl.pallas_call(
        _attn_kernel,
        out_shape=jax.ShapeDtypeStruct((N_PAIRS, batch, seq, HEAD_PAIR), F32),
        grid=(batch, N_PAIRS),
        in_specs=in_specs,
        out_specs=pl.BlockSpec((None, None, seq, HEAD_PAIR), lambda b, p: (p, b, 0, 0)),
        scratch_shapes=[scratch, scratch],
        compiler_params=pltpu.CompilerParams(
            dimension_semantics=("arbitrary", "arbitrary"), vmem_limit_bytes=VMEM_LIMIT),
        name="dilated_attention",
    )(*args)
    return out.reshape(N_PAIRS, batch * seq, HEAD_PAIR)


def _attention_bias():
    n = ATT_BLOCK
    heads = jnp.arange(1, N_QKV_HEADS + 1, dtype=F32)
    slopes = jnp.exp2(-ALIBI_MAX_BIAS * heads / N_QKV_HEADS).reshape(N_GROUPS, HEADS_PER_GROUP)
    qi = jnp.arange(n)[:, None]
    ki = jnp.arange(2 * n)[None, :]
    tables = []
    for g, (window, dil) in enumerate(ATT_GROUPS):
        steps = n + qi - ki
        valid = (steps >= 0) & (steps <= n)
        steps_f = qi - ki
        valid_f = steps_f >= 0
        per_head = []
        for hh in range(HEADS_PER_GROUP):
            sl = slopes[g, hh]
            reg = jnp.where(valid, -sl * (dil * steps).astype(F32), NEG_INF)
            fst = jnp.where(valid_f, -sl * (dil * steps_f).astype(F32), NEG_INF)
            per_head.append(jnp.stack([reg, fst]))
        tables.append(jnp.stack(per_head))
    return jnp.stack(tables).astype(F32)


def _block_diag(w):
    depth = w.shape[0]
    per = LRU_GATE_BLK // LRU_BLOCK
    w = w.reshape(depth, D_LRU // LRU_GATE_BLK, per, LRU_BLOCK, LRU_BLOCK)
    eye = jnp.eye(per, dtype=w.dtype)
    full = w[:, :, :, :, None, :] * eye[None, None, :, None, :, None]
    return full.reshape(depth, D_LRU // LRU_GATE_BLK, LRU_GATE_BLK, LRU_GATE_BLK)


def kernel(x, ffn1_norm, ffn1_w_gate, ffn1_w_up, ffn1_w_down, mix_norm, w_in, q_norm, k_norm, conv_w, conv_b, gate_a_w, gate_a_b, gate_x_w, gate_x_b, lru_lambda, w_out, ffn2_norm, ffn2_w_gate, ffn2_w_up, ffn2_w_down):
    batch, seq, _ = x.shape
    depth = w_in.shape[0]
    n = batch * seq

    w_in_b = w_in.astype(BF16)
    w_gates = (0.5 * jnp.concatenate([_block_diag(gate_a_w), _block_diag(gate_x_w)],
                                     axis=-1)).astype(BF16)
    gate_a_b = 0.5 * gate_a_b
    gate_x_b = 0.5 * gate_x_b
    head_id = jnp.arange(MXU_DIM) // HEAD_DIM
    head_mean = ((head_id[:, None] == head_id[None, :]).astype(F32) / HEAD_DIM).astype(BF16)
    score_scale = 1.0 / math.sqrt(HEAD_DIM)
    q_gain = (jnp.tile(q_norm, (1, N_QKV_HEADS)) * score_scale).reshape(depth, 1, D_ATT_QKV)
    k_gain = jnp.tile(k_norm, (1, N_QKV_HEADS)).reshape(depth, 1, D_ATT_QKV)
    bias = _attention_bias()

    def rows(v):
        return v.reshape(depth, 1, v.shape[-1])

    mixer_params = (rows(mix_norm), w_in_b, head_mean, q_gain, k_gain, conv_w, rows(conv_b),
                    w_gates, rows(gate_a_b), rows(gate_x_b), rows(lru_lambda))
    xf = x.reshape(n, D_MODEL)
    for l in range(depth):
        xf = _ffn(xf, l, ffn1_norm, ffn1_w_gate, ffn1_w_up, ffn1_w_down)
        qkv0, qkv1, qkv2, y = _mixer_in(xf, l, mixer_params, batch, seq)
        att = _attention((qkv0, qkv1, qkv2), bias, batch, seq)
        xf = _ffn(xf, l, ffn2_norm, ffn2_w_gate, ffn2_w_up, ffn2_w_down,
                  mixer_out=(att, y, w_out))
    return xf.reshape(batch, seq, D_MODEL)
```

```python
import functools
import math

import jax
import jax.numpy as jnp
from jax import lax
from jax.experimental import pallas as pl
from jax.experimental.pallas import tpu as pltpu

F32 = jnp.float32
BF16 = jnp.bfloat16

D_MODEL = 1024
HEAD_DIM = 64
ATT_GROUPS = ((128, 1), (512, 4), (2048, 16))
N_GROUPS = len(ATT_GROUPS)
HEADS_PER_GROUP = 4
N_QKV_HEADS = N_GROUPS * HEADS_PER_GROUP
D_ATT_QKV = N_QKV_HEADS * HEAD_DIM
D_ATT_OUT = HEADS_PER_GROUP * HEAD_DIM
ALIBI_MAX_BIAS = 8.0
D_LRU = 768
LRU_BLOCK = 64
CONV_WIDTH = 4
LRU_C = 8.0
D_FF = 2816
FFN_RES_WEIGHT = 0.5
RMS_EPS = 1e-6
NEG_INF = -1e30

LANES = 128
SUBLANES = 8
MXU_DIM = 256
VMEM_LIMIT = 56 * 1024 * 1024

FF_CHUNK = MXU_DIM
N_FF_CHUNKS = D_FF // FF_CHUNK
TM_FFN = 512
TM_PROJ = 512
ATT_BLOCK = 128
HEAD_PAIR = LANES
N_PAIRS = D_ATT_OUT // HEAD_PAIR
MAX_DIL = ATT_GROUPS[-1][1]
ATT_UNROLL = 16
HEADS_PER_MXU = MXU_DIM // HEAD_DIM
LRU_GATE_BLK = MXU_DIM
N_LRU_LANE_TILES = D_LRU // LANES
LRU_SEG = TM_PROJ // SUBLANES
LRU_PITCH = LRU_SEG + SUBLANES
CLS16_PITCH = TM_PROJ // MAX_DIL + SUBLANES


def _resident(block_shape, index_map):
    return pl.BlockSpec(block_shape, index_map, pipeline_mode=pl.Buffered(1))


def _rms_norm(x, gain):
    ms = jnp.mean(x * x, axis=-1, keepdims=True)
    return x * lax.rsqrt(ms + RMS_EPS) * gain


def _ffn_kernel(*refs, with_mixer_out):
    if with_mixer_out:
        x_ref, att_ref, y_ref, wo_ref, gain_ref, wg_ref, wu_ref, wd_ref, o_ref, acc_ref = refs
        mix = jnp.concatenate(
            [att_ref[p].astype(BF16) for p in range(N_PAIRS)] + [y_ref[...]], axis=-1)
        x = x_ref[...] + jnp.dot(mix, wo_ref[...].astype(BF16),
                                 preferred_element_type=F32)
    else:
        x_ref, gain_ref, wg_ref, wu_ref, wd_ref, o_ref, acc_ref = refs
        x = x_ref[...]
    h = _rms_norm(x, gain_ref[...]).astype(BF16)
    for j in range(N_FF_CHUNKS):
        cols = slice(j * FF_CHUNK, (j + 1) * FF_CHUNK)
        g = jnp.dot(h, wg_ref[:, cols].astype(BF16), preferred_element_type=F32)
        u = jnp.dot(h, wu_ref[:, cols].astype(BF16), preferred_element_type=F32)
        a = (g / (1.0 + jnp.exp(-g)) * u).astype(BF16)
        d = jnp.dot(a, wd_ref[cols, :].astype(BF16), preferred_element_type=F32)
        if j == 0:
            acc_ref[...] = d
        else:
            acc_ref[...] += d
    o_ref[...] = x + FFN_RES_WEIGHT * acc_ref[...]


def _ffn(x, layer, gain, wg, wu, wd, mixer_out=None):
    n = x.shape[0]
    row_tile = pl.BlockSpec((TM_FFN, D_MODEL), lambda i: (i, 0))

    def layer_slice(rows, cols):
        return _resident((None, rows, cols), lambda i: (layer, 0, 0))

    in_specs, args = [row_tile], [x]
    if mixer_out is not None:
        att, y, w_out = mixer_out
        in_specs += [
            pl.BlockSpec((N_PAIRS, TM_FFN, HEAD_PAIR), lambda i: (0, i, 0)),
            pl.BlockSpec((TM_FFN, D_LRU), lambda i: (i, 0)),
            layer_slice(D_ATT_OUT + D_LRU, D_MODEL),
        ]
        args += [att, y, w_out]
    in_specs += [
        layer_slice(1, D_MODEL),
        layer_slice(D_MODEL, D_FF),
        layer_slice(D_MODEL, D_FF),
        layer_slice(D_FF, D_MODEL),
    ]
    args += [gain.reshape(gain.shape[0], 1, D_MODEL), wg, wu, wd]
    return pl.pallas_call(
        functools.partial(_ffn_kernel, with_mixer_out=mixer_out is not None),
        out_shape=jax.ShapeDtypeStruct((n, D_MODEL), F32),
        grid=(n // TM_FFN,),
        in_specs=in_specs,
        out_specs=row_tile,
        scratch_shapes=[pltpu.VMEM((TM_FFN, D_MODEL), F32)],
        compiler_params=pltpu.CompilerParams(
            dimension_semantics=("arbitrary",), vmem_limit_bytes=VMEM_LIMIT),
        name="ffn_mixer_out" if mixer_out is not None else "ffn",
    )(*args)


def _mixer_in_kernel(x_ref, gain_ref, w_ref, hm_ref, qg_ref, kg_ref,
                     cw_ref, cb_ref, wgate_ref, ba_ref, bx_ref, lam_ref,
                     o0_ref, o1_ref, o2_ref, y_ref,
                     slab_ref, slab16_ref, seg_ref, hist_ref, h_ref):
    @pl.when(pl.program_id(1) == 0)
    def _():
        hist_ref[...] = jnp.zeros(hist_ref.shape, F32)
        h_ref[...] = jnp.zeros(h_ref.shape, F32)

    x = x_ref[...]
    h = _rms_norm(x, gain_ref[...]).astype(BF16)

    def head_norm(t, gain):
        sq = (t * t).astype(BF16)
        ms = jnp.concatenate(
            [jnp.dot(sq[:, i * MXU_DIM:(i + 1) * MXU_DIM], hm_ref[...],
                     preferred_element_type=F32)
             for i in range(D_ATT_QKV // MXU_DIM)], axis=-1)
        return t * lax.rsqrt(ms + RMS_EPS) * gain

    def zero_after(val):
        tail = val[val.shape[0] - SUBLANES:, :]
        bits = None
        for j in range(val.shape[1] // LANES):
            piece = lax.bitcast_convert_type(tail[:, j * LANES:(j + 1) * LANES], jnp.uint32)
            bits = piece if bits is None else bits | piece
        bits = lax.shift_right_logical(lax.shift_right_logical(bits, jnp.uint32(16)),
                                       jnp.uint32(16))
        return bits[0:1, :]

    def ordered_after(row, zero_bits):
        tiled = jnp.concatenate([zero_bits] * (row.shape[1] // LANES), axis=-1)
        return lax.bitcast_convert_type(
            lax.bitcast_convert_type(row, jnp.uint32) | tiled, F32)

    c0 = 3 * D_ATT_QKV
    lru_x = jnp.dot(h, w_ref[:, c0:c0 + D_LRU], preferred_element_type=F32)
    gate = jnp.dot(h, w_ref[:, c0 + D_LRU:c0 + 2 * D_LRU], preferred_element_type=F32)
    q = jnp.dot(h, w_ref[:, 0:D_ATT_QKV], preferred_element_type=F32)

    sub = lax.broadcasted_iota(jnp.int32, (SUBLANES, LANES), 0)
    seg_rows = [slice(i * SUBLANES, (i + 1) * SUBLANES) for i in range(LRU_SEG)]

    def to_segment_order(col, lt):
        for s in range(SUBLANES):
            seg_ref[lt, pl.ds(s * LRU_PITCH, LRU_SEG), :] = col[s * LRU_SEG:(s + 1) * LRU_SEG]
        return [seg_ref[lt, pl.ds(i, SUBLANES, stride=LRU_PITCH), :] for i in range(LRU_SEG)]

    def to_time_order(vregs, lt):
        for i, vreg in enumerate(vregs):
            seg_ref[lt, pl.ds(i, SUBLANES, stride=LRU_PITCH), :] = vreg
        return jnp.concatenate(
            [seg_ref[lt, pl.ds(s * LRU_PITCH, LRU_SEG), :] for s in range(SUBLANES)], axis=0)

    n_hist = CONV_WIDTH - 1
    xc_cols = []
    for lt in range(N_LRU_LANE_TILES):
        lanes = slice(lt * LANES, (lt + 1) * LANES)
        xv = to_segment_order(lru_x[:, lanes], lt)
        head = []
        for j in range(n_hist):
            last = xv[LRU_SEG - n_hist + j]
            prev = hist_ref[lt, seg_rows[j], :]
            head.append(jnp.where(sub == 0, pltpu.roll(prev, 1, 0), pltpu.roll(last, 1, 0)))
            hist_ref[lt, seg_rows[j], :] = last
        ext = jnp.concatenate(head + xv, axis=0)
        xc_l = cb_ref[:, lanes]
        for w in range(CONV_WIDTH):
            xc_l = xc_l + cw_ref[w:w + 1, lanes] * ext[w * SUBLANES:w * SUBLANES + TM_PROJ]
        xc_cols.append(xc_l)
    xc = jnp.concatenate(xc_cols, axis=-1)

    xcb = xc.astype(BF16)
    gates = [jnp.dot(xcb[:, i * LRU_GATE_BLK:(i + 1) * LRU_GATE_BLK], wgate_ref[i],
                     preferred_element_type=F32)
             for i in range(D_LRU // LRU_GATE_BLK)]
    k = jnp.dot(h, w_ref[:, D_ATT_QKV:2 * D_ATT_QKV], preferred_element_type=F32)
    v = jnp.dot(h, w_ref[:, 2 * D_ATT_QKV:3 * D_ATT_QKV], preferred_element_type=F32)
    bias_a = ordered_after(ba_ref[...], zero_after(q))
    tanh_a = jnp.tanh(jnp.concatenate([t[:, :LRU_GATE_BLK] for t in gates], axis=-1) + bias_a)
    tanh_x = jnp.tanh(jnp.concatenate([t[:, LRU_GATE_BLK:] for t in gates], axis=-1)
                      + bx_ref[...])
    z = -lam_ref[...]
    softplus = jnp.maximum(z, 0.0) + jnp.log1p(jnp.exp(-jnp.abs(z)))
    half_decay = (-0.5 * LRU_C * math.log2(math.e)) * softplus
    a = jnp.exp2(half_decay + half_decay * tanh_a)
    half_xc = 0.5 * xc
    s = 1.0 - a * a
    root = jnp.where(s > 0.0, s * lax.rsqrt(s), 0.0)
    b = root * (half_xc + half_xc * tanh_x)

    zero_k = zero_after(k)
    zero_v = zero_after(v)
    h_cols = []
    for lt in range(N_LRU_LANE_TILES):
        lanes = slice(lt * LANES, (lt + 1) * LANES)
        a_l, b_l = a[:, lanes], b[:, lanes]
        h_loc = jnp.zeros((SUBLANES, LANES), F32)
        a_cum = jnp.ones((SUBLANES, LANES), F32)
        h_locs, a_cums = [], []
        for rows in seg_rows:
            h_loc = a_l[rows] * h_loc + b_l[rows]
            a_cum = a_cum * a_l[rows]
            h_locs.append(h_loc)
            a_cums.append(a_cum)
        carry = ordered_after(h_ref[lt, 0:1, :],
                              zero_k if lt < N_LRU_LANE_TILES // 2 else zero_v)
        carry_in = jnp.zeros((SUBLANES, LANES), F32)
        for s in range(SUBLANES):
            carry_in = jnp.where(sub == s, carry, carry_in)
            carry = a_cum[s:s + 1, :] * carry + h_loc[s:s + 1, :]
        h_ref[lt, 0:1, :] = carry
        h_cols.append(to_time_order(
            [hl + ac * carry_in for hl, ac in zip(h_locs, a_cums)], lt))
    h_all = jnp.concatenate(h_cols, axis=-1)

    inner = math.sqrt(2.0 / math.pi) * (gate + 0.044715 * (gate * gate * gate))
    gelu = 0.5 * gate * (1.0 + jnp.tanh(inner))
    y_ref[...] = (h_all * gelu).astype(y_ref.dtype)

    qn = head_norm(q, qg_ref[...])
    kn = head_norm(k, kg_ref[...])
    outs = (o0_ref, o1_ref, o2_ref)
    for g, (window, dil) in enumerate(ATT_GROUPS):
        for part, val in enumerate((qn, kn, v)):
            for p in range(N_PAIRS):
                c = g * D_ATT_OUT + p * HEAD_PAIR
                piece = val[:, c:c + HEAD_PAIR]
                dst = outs[g].at[part * N_PAIRS + p]
                slab = part * N_PAIRS + p
                if dil == 1:
                    dst[0] = piece.astype(BF16)
                elif dil == SUBLANES * 2:
                    per_class = TM_PROJ // dil
                    for i in range(TM_PROJ // SUBLANES):
                        start = (i % 2) * SUBLANES * CLS16_PITCH + i // 2
                        slab16_ref[slab, pl.ds(start, SUBLANES, stride=CLS16_PITCH), :] = (
                            piece[i * SUBLANES:(i + 1) * SUBLANES])
                    for cls in range(dil):
                        dst[cls] = slab16_ref[
                            slab, pl.ds(cls * CLS16_PITCH, per_class), :].astype(BF16)
                else:
                    slab_ref[slab] = piece
                    for cls in range(dil):
                        rows = pl.ds(cls, TM_PROJ // dil, stride=dil)
                        dst[cls] = slab_ref[slab, rows, :].astype(BF16)


def _mixer_in(x, layer, params, batch, seq):
    n = x.shape[0]
    tiles_per_seq = seq // TM_PROJ

    def token_tile(width):
        return pl.BlockSpec((TM_PROJ, width), lambda b, t: (b * tiles_per_seq + t, 0))

    def param_spec(p):
        if p.ndim == 2:
            return _resident(p.shape, lambda b, t: (0, 0))
        zeros = (0,) * (p.ndim - 1)
        return _resident((None, *p.shape[1:]), lambda b, t: (layer, *zeros))

    qkv_shapes, qkv_specs = [], []
    for window, dil in ATT_GROUPS:
        qkv_shapes.append(jax.ShapeDtypeStruct(
            (3 * N_PAIRS, batch, dil, seq // dil, HEAD_PAIR), BF16))
        qkv_specs.append(pl.BlockSpec(
            (3 * N_PAIRS, None, dil, TM_PROJ // dil, HEAD_PAIR),
            lambda b, t: (0, b, 0, t, 0)))
    return pl.pallas_call(
        _mixer_in_kernel,
        out_shape=(*qkv_shapes, jax.ShapeDtypeStruct((n, D_LRU), BF16)),
        grid=(batch, tiles_per_seq),
        in_specs=[token_tile(D_MODEL)] + [param_spec(p) for p in params],
        out_specs=(*qkv_specs, token_tile(D_LRU)),
        scratch_shapes=[
            pltpu.VMEM((3 * N_PAIRS, TM_PROJ, HEAD_PAIR), F32),
            pltpu.VMEM((3 * N_PAIRS, MAX_DIL * CLS16_PITCH, HEAD_PAIR), F32),
            pltpu.VMEM((N_LRU_LANE_TILES, SUBLANES * LRU_PITCH, LANES), F32),
            pltpu.VMEM((N_LRU_LANE_TILES, (CONV_WIDTH - 1) * SUBLANES, LANES), F32),
            pltpu.VMEM((N_LRU_LANE_TILES, SUBLANES, LANES), F32),
        ],
        compiler_params=pltpu.CompilerParams(
            dimension_semantics=("arbitrary", "arbitrary"), vmem_limit_bytes=VMEM_LIMIT),
        name="mixer_in",
    )(x, *params)


def _attn_kernel(q0_ref, k0_ref, v0_ref, q1_ref, k1_ref, v1_ref,
                 q2_ref, k2_ref, v2_ref, bias_ref, out_ref, so_ref, sm_ref, sl_ref):
    qkv = ((q0_ref, k0_ref, v0_ref), (q1_ref, k1_ref, v1_ref), (q2_ref, k2_ref, v2_ref))
    seq = out_ref.shape[0]
    n_blocks = seq // ATT_BLOCK
    dense_pitch = seq // MAX_DIL + SUBLANES
    lane = lax.broadcasted_iota(jnp.int32, (ATT_BLOCK, HEAD_PAIR), 1)
    head0 = lane < HEAD_DIM

    def block(g, idx, blocks_per_class):
        q_ref, k_ref, v_ref = qkv[g]
        first = (idx & (blocks_per_class - 1)) == 0
        q_start = pl.multiple_of(idx * ATT_BLOCK, ATT_BLOCK)
        k_start = pl.multiple_of(jnp.where(first, idx, idx - 1) * ATT_BLOCK, ATT_BLOCK)
        table = jnp.where(first, 1, 0)
        q = q_ref[pl.ds(q_start, ATT_BLOCK), :]
        k = k_ref[pl.ds(k_start, 2 * ATT_BLOCK), :]
        v = v_ref[pl.ds(k_start, 2 * ATT_BLOCK), :]
        zero = jnp.zeros_like(q)
        q2 = jnp.concatenate([jnp.where(head0, q, zero), jnp.where(head0, zero, q)], axis=0)
        s2 = lax.dot_general(q2, k, (((1,), (1,)), ((), ())), preferred_element_type=F32)
        es, ms, ls = [], [], []
        for hh in range(2):
            s = s2[hh * ATT_BLOCK:(hh + 1) * ATT_BLOCK] + bias_ref[g, hh, table]
            m = jnp.max(s, axis=-1, keepdims=True)
            e = jnp.exp(s - m)
            ls.append(jnp.sum(e, axis=-1, keepdims=True))
            ms.append(m)
            es.append(e.astype(BF16))
        o2 = jnp.dot(jnp.concatenate(es, axis=0), v, preferred_element_type=F32)
        stats = (jnp.where(head0, o2[:ATT_BLOCK], o2[ATT_BLOCK:]),
                 jnp.where(head0, ms[0], ms[1]),
                 jnp.where(head0, ls[0], ls[1]))
        for ref, val in zip((so_ref, sm_ref, sl_ref), stats):
            if ATT_GROUPS[g][1] > 1:
                ref[g, pl.ds(q_start, ATT_BLOCK), :] = val
                continue
            for u in range(ATT_BLOCK // SUBLANES):
                start = (u % 2) * SUBLANES * dense_pitch + idx * (ATT_BLOCK // MAX_DIL) + u // 2
                ref[g, pl.ds(start, SUBLANES, stride=dense_pitch), :] = (
                    val[u * SUBLANES:(u + 1) * SUBLANES])

    for g, (window, dil) in enumerate(ATT_GROUPS):
        blocks_per_class = n_blocks // dil

        def body(it, carry, g=g, blocks_per_class=blocks_per_class):
            for u in range(ATT_UNROLL):
                block(g, it * ATT_UNROLL + u, blocks_per_class)
            return carry

        lax.fori_loop(0, n_blocks // ATT_UNROLL, body, 0)

    rows = seq // MAX_DIL
    for c16 in range(MAX_DIL):
        idxs = []
        for g, (window, dil) in enumerate(ATT_GROUPS):
            if dil == 1:
                idxs.append(pl.ds(c16 * dense_pitch, rows))
            elif dil == MAX_DIL:
                idxs.append(pl.ds(c16 * rows, rows))
            else:
                idxs.append(pl.ds((c16 % dil) * (seq // dil) + c16 // dil, rows,
                                  stride=MAX_DIL // dil))
        ms_ = [sm_ref[g, idx, :] for g, idx in enumerate(idxs)]
        m = jnp.maximum(jnp.maximum(ms_[0], ms_[1]), ms_[2])
        ws = [jnp.exp(mg - m) for mg in ms_]
        num = sum(w * so_ref[g, idx, :] for g, (w, idx) in enumerate(zip(ws, idxs)))
        den = sum(w * sl_ref[g, idx, :] for g, (w, idx) in enumerate(zip(ws, idxs)))
        out_ref[pl.ds(c16, rows, stride=MAX_DIL), :] = num / den


def _attention(qkv_groups, bias, batch, seq):
    in_specs, args = [], []
    for g, (window, dil) in enumerate(ATT_GROUPS):
        arr = qkv_groups[g].reshape(3 * N_PAIRS, batch, seq, HEAD_PAIR)
        for part in range(3):
            in_specs.append(pl.BlockSpec(
                (None, None, seq, HEAD_PAIR),
                lambda b, p, part=part: (part * N_PAIRS + p, b, 0, 0)))
            args.append(arr)
    in_specs.append(pl.BlockSpec(
        (N_GROUPS, 2, 2, ATT_BLOCK, 2 * ATT_BLOCK), lambda b, p: (0, p, 0, 0, 0)))
    args.append(bias)
    scratch = pltpu.VMEM((N_GROUPS, seq + MAX_DIL * SUBLANES, HEAD_PAIR), F32)
    out = pl.pallas_call(
        _attn_kernel,
        out_shape=jax.ShapeDtypeStruct((N_PAIRS, batch, seq, HEAD_PAIR), F32),
        grid=(batch, N_PAIRS),
        in_specs=in_specs,
        out_specs=pl.BlockSpec((None, None, seq, HEAD_PAIR), lambda b, p: (p, b, 0, 0)),
        scratch_shapes=[scratch, scratch, scratch],
        compiler_params=pltpu.CompilerParams(
            dimension_semantics=("arbitrary", "arbitrary"), vmem_limit_bytes=VMEM_LIMIT),
        name="dilated_attention",
    )(*args)
    return out.reshape(N_PAIRS, batch * seq, HEAD_PAIR)


def _attention_bias():
    n = ATT_BLOCK
    heads = jnp.arange(1, N_QKV_HEADS + 1, dtype=F32)
    slopes = jnp.exp2(-ALIBI_MAX_BIAS * heads / N_QKV_HEADS).reshape(N_GROUPS, HEADS_PER_GROUP)
    qi = jnp.arange(n)[:, None]
    ki = jnp.arange(2 * n)[None, :]
    tables = []
    for g, (window, dil) in enumerate(ATT_GROUPS):
        steps = n + qi - ki
        valid = (steps >= 0) & (steps <= n)
        steps_f = qi - ki
        valid_f = steps_f >= 0
        per_head = []
        for hh in range(HEADS_PER_GROUP):
            sl = slopes[g, hh]
            reg = jnp.where(valid, -sl * (dil * steps).astype(F32), NEG_INF)
            fst = jnp.where(valid_f, -sl * (dil * steps_f).astype(F32), NEG_INF)
            per_head.append(jnp.stack([reg, fst]))
        tables.append(jnp.stack(per_head))
    return jnp.stack(tables).astype(F32)


def _block_diag(w):
    depth = w.shape[0]
    per = LRU_GATE_BLK // LRU_BLOCK
    w = w.reshape(depth, D_LRU // LRU_GATE_BLK, per, LRU_BLOCK, LRU_BLOCK)
    eye = jnp.eye(per, dtype=w.dtype)
    full = w[:, :, :, :, None, :] * eye[None, None, :, None, :, None]
    return full.reshape(depth, D_LRU // LRU_GATE_BLK, LRU_GATE_BLK, LRU_GATE_BLK)


def kernel(x, ffn1_norm, ffn1_w_gate, ffn1_w_up, ffn1_w_down, mix_norm, w_in, q_norm, k_norm, conv_w, conv_b, gate_a_w, gate_a_b, gate_x_w, gate_x_b, lru_lambda, w_out, ffn2_norm, ffn2_w_gate, ffn2_w_up, ffn2_w_down):
    batch, seq, _ = x.shape
    depth = w_in.shape[0]
    n = batch * seq

    w_in_b = w_in.astype(BF16)
    w_gates = (0.5 * jnp.concatenate([_block_diag(gate_a_w), _block_diag(gate_x_w)],
                                     axis=-1)).astype(BF16)
    gate_a_b = 0.5 * gate_a_b
    gate_x_b = 0.5 * gate_x_b
    head_id = jnp.arange(MXU_DIM) // HEAD_DIM
    head_mean = ((head_id[:, None] == head_id[None, :]).astype(F32) / HEAD_DIM).astype(BF16)
    score_scale = 1.0 / math.sqrt(HEAD_DIM)
    q_gain = (jnp.tile(q_norm, (1, N_QKV_HEADS)) * score_scale).reshape(depth, 1, D_ATT_QKV)
    k_gain = jnp.tile(k_norm, (1, N_QKV_HEADS)).reshape(depth, 1, D_ATT_QKV)
    bias = _attention_bias()

    def rows(v):
        return v.reshape(depth, 1, v.shape[-1])

    mixer_params = (rows(mix_norm), w_in_b, head_mean, q_gain, k_gain, conv_w, rows(conv_b),
                    w_gates, rows(gate_a_b), rows(gate_x_b), rows(lru_lambda))
    xf = x.reshape(n, D_MODEL)
    for l in range(depth):
        xf = _ffn(xf, l, ffn1_norm, ffn1_w_gate, ffn1_w_up, ffn1_w_down)
        qkv0, qkv1, qkv2, y = _mixer_in(xf, l, mixer_params, batch, seq)
        att = _attention((qkv0, qkv1, qkv2), bias, batch, seq)
        xf = _ffn(xf, l, ffn2_norm, ffn2_w_gate, ffn2_w_up, ffn2_w_down,
                  mixer_out=(att, y, w_out))
    return xf.reshape(batch, seq, D_MODEL)
```

```python
import functools
import math

import jax
import jax.numpy as jnp
from jax import lax
from jax.experimental import pallas as pl
from jax.experimental.pallas import tpu as pltpu

F32 = jnp.float32
BF16 = jnp.bfloat16

D_MODEL = 1024
HEAD_DIM = 64
ATT_GROUPS = ((128, 1), (512, 4), (2048, 16))
N_GROUPS = len(ATT_GROUPS)
HEADS_PER_GROUP = 4
N_QKV_HEADS = N_GROUPS * HEADS_PER_GROUP
D_ATT_QKV = N_QKV_HEADS * HEAD_DIM
D_ATT_OUT = HEADS_PER_GROUP * HEAD_DIM
ALIBI_MAX_BIAS = 8.0
D_LRU = 768
LRU_BLOCK = 64
CONV_WIDTH = 4
LRU_C = 8.0
D_FF = 2816
FFN_RES_WEIGHT = 0.5
RMS_EPS = 1e-6
NEG_INF = -1e30

LANES = 128
SUBLANES = 8
MXU_DIM = 256
VMEM_LIMIT = 56 * 1024 * 1024

FF_CHUNK = MXU_DIM
N_FF_CHUNKS = D_FF // FF_CHUNK
TM_FFN = 512
TM_PROJ = 512
ATT_BLOCK = 128
HEAD_PAIR = LANES
N_PAIRS = D_ATT_OUT // HEAD_PAIR
MAX_DIL = ATT_GROUPS[-1][1]
ATT_UNROLL = 16
HEADS_PER_MXU = MXU_DIM // HEAD_DIM
LRU_GATE_BLK = MXU_DIM
N_LRU_LANE_TILES = D_LRU // LANES
LRU_SEG = TM_PROJ // SUBLANES
LRU_PITCH = LRU_SEG + SUBLANES
CLS16_PITCH = TM_PROJ // MAX_DIL + SUBLANES


def _resident(block_shape, index_map):
    return pl.BlockSpec(block_shape, index_map, pipeline_mode=pl.Buffered(1))


def _rms_norm(x, gain):
    ms = jnp.mean(x * x, axis=-1, keepdims=True)
    return x * lax.rsqrt(ms + RMS_EPS) * gain


def _ffn_kernel(*refs, with_mixer_out):
    if with_mixer_out:
        x_ref, att_ref, y_ref, wo_ref, gain_ref, wg_ref, wu_ref, wd_ref, o_ref, acc_ref = refs
        mix = jnp.concatenate(
            [att_ref[p].astype(BF16) for p in range(N_PAIRS)] + [y_ref[...]], axis=-1)
        x = x_ref[...] + jnp.dot(mix, wo_ref[...].astype(BF16),
                                 preferred_element_type=F32)
    else:
        x_ref, gain_ref, wg_ref, wu_ref, wd_ref, o_ref, acc_ref = refs
        x = x_ref[...]
    h = _rms_norm(x, gain_ref[...]).astype(BF16)
    for j in range(N_FF_CHUNKS):
        cols = slice(j * FF_CHUNK, (j + 1) * FF_CHUNK)
        g = jnp.dot(h, wg_ref[:, cols].astype(BF16), preferred_element_type=F32)
        u = jnp.dot(h, wu_ref[:, cols].astype(BF16), preferred_element_type=F32)
        a = (g / (1.0 + jnp.exp(-g)) * u).astype(BF16)
        d = jnp.dot(a, wd_ref[cols, :].astype(BF16), preferred_element_type=F32)
        if j == 0:
            acc_ref[...] = d
        else:
            acc_ref[...] += d
    o_ref[...] = x + FFN_RES_WEIGHT * acc_ref[...]


def _ffn(x, layer, gain, wg, wu, wd, mixer_out=None):
    n = x.shape[0]
    row_tile = pl.BlockSpec((TM_FFN, D_MODEL), lambda i: (i, 0))

    def layer_slice(rows, cols):
        return _resident((None, rows, cols), lambda i: (layer, 0, 0))

    in_specs, args = [row_tile], [x]
    if mixer_out is not None:
        att, y, w_out = mixer_out
        in_specs += [
            pl.BlockSpec((N_PAIRS, TM_FFN, HEAD_PAIR), lambda i: (0, i, 0)),
            pl.BlockSpec((TM_FFN, D_LRU), lambda i: (i, 0)),
            layer_slice(D_ATT_OUT + D_LRU, D_MODEL),
        ]
        args += [att, y, w_out]
    in_specs += [
        layer_slice(1, D_MODEL),
        layer_slice(D_MODEL, D_FF),
        layer_slice(D_MODEL, D_FF),
        layer_slice(D_FF, D_MODEL),
    ]
    args += [gain.reshape(gain.shape[0], 1, D_MODEL), wg, wu, wd]
    return pl.pallas_call(
        functools.partial(_ffn_kernel, with_mixer_out=mixer_out is not None),
        out_shape=jax.ShapeDtypeStruct((n, D_MODEL), F32),
        grid=(n // TM_FFN,),
        in_specs=in_specs,
        out_specs=row_tile,
        scratch_shapes=[pltpu.VMEM((TM_FFN, D_MODEL), F32)],
        compiler_params=pltpu.CompilerParams(
            dimension_semantics=("arbitrary",), vmem_limit_bytes=VMEM_LIMIT),
        name="ffn_mixer_out" if mixer_out is not None else "ffn",
    )(*args)


def _mixer_in_kernel(x_ref, gain_ref, w_ref, hm_ref, qg_ref, kg_ref,
                     cw_ref, cb_ref, wgate_ref, ba_ref, bx_ref, lam_ref,
                     o0_ref, o1_ref, o2_ref, y_ref,
                     slab_ref, slab16_ref, seg_ref, hist_ref, h_ref):
    @pl.when(pl.program_id(1) == 0)
    def _():
        hist_ref[...] = jnp.zeros(hist_ref.shape, F32)
        h_ref[...] = jnp.zeros(h_ref.shape, F32)

    x = x_ref[...]
    h = _rms_norm(x, gain_ref[...]).astype(BF16)

    def head_norm(t, gain):
        sq = (t * t).astype(BF16)
        ms = jnp.concatenate(
            [jnp.dot(sq[:, i * MXU_DIM:(i + 1) * MXU_DIM], hm_ref[...],
                     preferred_element_type=F32)
             for i in range(D_ATT_QKV // MXU_DIM)], axis=-1)
        return t * lax.rsqrt(ms + RMS_EPS) * gain

    c0 = 3 * D_ATT_QKV
    lru_x = jnp.dot(h, w_ref[:, c0:c0 + D_LRU], preferred_element_type=F32)
    q = jnp.dot(h, w_ref[:, 0:D_ATT_QKV], preferred_element_type=F32)

    sub = lax.broadcasted_iota(jnp.int32, (SUBLANES, LANES), 0)
    seg_rows = [slice(i * SUBLANES, (i + 1) * SUBLANES) for i in range(LRU_SEG)]

    def to_segment_order(col, lt):
        for s in range(SUBLANES):
            seg_ref[lt, pl.ds(s * LRU_PITCH, LRU_SEG), :] = col[s * LRU_SEG:(s + 1) * LRU_SEG]
        return [seg_ref[lt, pl.ds(i, SUBLANES, stride=LRU_PITCH), :] for i in range(LRU_SEG)]

    def to_time_order(vregs, lt):
        for i, vreg in enumerate(vregs):
            seg_ref[lt, pl.ds(i, SUBLANES, stride=LRU_PITCH), :] = vreg
        return jnp.concatenate(
            [seg_ref[lt, pl.ds(s * LRU_PITCH, LRU_SEG), :] for s in range(SUBLANES)], axis=0)

    n_hist = CONV_WIDTH - 1
    xc_cols = []
    for lt in range(N_LRU_LANE_TILES):
        lanes = slice(lt * LANES, (lt + 1) * LANES)
        xv = to_segment_order(lru_x[:, lanes], lt)
        head = []
        for j in range(n_hist):
            last = xv[LRU_SEG - n_hist + j]
            prev = hist_ref[lt, seg_rows[j], :]
            head.append(jnp.where(sub == 0, pltpu.roll(prev, 1, 0), pltpu.roll(last, 1, 0)))
            hist_ref[lt, seg_rows[j], :] = last
        ext = jnp.concatenate(head + xv, axis=0)
        xc_l = cb_ref[:, lanes]
        for w in range(CONV_WIDTH):
            xc_l = xc_l + cw_ref[w:w + 1, lanes] * ext[w * SUBLANES:w * SUBLANES + TM_PROJ]
        xc_cols.append(xc_l)
    xc = jnp.concatenate(xc_cols, axis=-1)

    xcb = xc.astype(BF16)
    gates = [jnp.dot(xcb[:, i * LRU_GATE_BLK:(i + 1) * LRU_GATE_BLK], wgate_ref[i],
                     preferred_element_type=F32)
             for i in range(D_LRU // LRU_GATE_BLK)]
    k = jnp.dot(h, w_ref[:, D_ATT_QKV:2 * D_ATT_QKV], preferred_element_type=F32)
    gate = jnp.dot(h, w_ref[:, c0 + D_LRU:c0 + 2 * D_LRU], preferred_element_type=F32)
    v = jnp.dot(h, w_ref[:, 2 * D_ATT_QKV:3 * D_ATT_QKV], preferred_element_type=F32)
    tanh_a = jnp.tanh(jnp.concatenate([t[:, :LRU_GATE_BLK] for t in gates], axis=-1)
                      + ba_ref[...])
    tanh_x = jnp.tanh(jnp.concatenate([t[:, LRU_GATE_BLK:] for t in gates], axis=-1)
                      + bx_ref[...])
    z = -lam_ref[...]
    softplus = jnp.maximum(z, 0.0) + jnp.log1p(jnp.exp(-jnp.abs(z)))
    half_decay = (-0.5 * LRU_C * math.log2(math.e)) * softplus
    a = jnp.exp2(half_decay + half_decay * tanh_a)
    half_xc = 0.5 * xc
    s = 1.0 - a * a
    root = jnp.where(s > 0.0, s * lax.rsqrt(s), 0.0)
    b = root * (half_xc + half_xc * tanh_x)

    h_cols = []
    for lt in range(N_LRU_LANE_TILES):
        lanes = slice(lt * LANES, (lt + 1) * LANES)
        a_l, b_l = a[:, lanes], b[:, lanes]
        h_loc = jnp.zeros((SUBLANES, LANES), F32)
        a_cum = jnp.ones((SUBLANES, LANES), F32)
        h_locs, a_cums = [], []
        for rows in seg_rows:
            h_loc = a_l[rows] * h_loc + b_l[rows]
            a_cum = a_cum * a_l[rows]
            h_locs.append(h_loc)
            a_cums.append(a_cum)
        carry = h_ref[lt, 0:1, :]
        carry_in = jnp.zeros((SUBLANES, LANES), F32)
        for s in range(SUBLANES):
            carry_in = jnp.where(sub == s, carry, carry_in)
            carry = a_cum[s:s + 1, :] * carry + h_loc[s:s + 1, :]
        h_ref[lt, 0:1, :] = carry
        h_cols.append(to_time_order(
            [hl + ac * carry_in for hl, ac in zip(h_locs, a_cums)], lt))
    h_all = jnp.concatenate(h_cols, axis=-1)

    inner = math.sqrt(2.0 / math.pi) * (gate + 0.044715 * (gate * gate * gate))
    gelu = 0.5 * gate * (1.0 + jnp.tanh(inner))
    y_ref[...] = (h_all * gelu).astype(y_ref.dtype)

    qn = head_norm(q, qg_ref[...])
    kn = head_norm(k, kg_ref[...])
    outs = (o0_ref, o1_ref, o2_ref)
    for g, (window, dil) in enumerate(ATT_GROUPS):
        for part, val in enumerate((qn, kn, v)):
            for p in range(N_PAIRS):
                c = g * D_ATT_OUT + p * HEAD_PAIR
                piece = val[:, c:c + HEAD_PAIR]
                dst = outs[g].at[part * N_PAIRS + p]
                slab = part * N_PAIRS + p
                if dil == 1:
                    dst[0] = piece.astype(BF16)
                elif dil == SUBLANES * 2:
                    per_class = TM_PROJ // dil
                    for i in range(TM_PROJ // SUBLANES):
                        start = (i % 2) * SUBLANES * CLS16_PITCH + i // 2
                        slab16_ref[slab, pl.ds(start, SUBLANES, stride=CLS16_PITCH), :] = (
                            piece[i * SUBLANES:(i + 1) * SUBLANES])
                    for cls in range(dil):
                        dst[cls] = slab16_ref[
                            slab, pl.ds(cls * CLS16_PITCH, per_class), :].astype(BF16)
                else:
                    slab_ref[slab] = piece
                    for cls in range(dil):
                        rows = pl.ds(cls, TM_PROJ // dil, stride=dil)
                        dst[cls] = slab_ref[slab, rows, :].astype(BF16)


def _mixer_in(x, layer, params, batch, seq):
    n = x.shape[0]
    tiles_per_seq = seq // TM_PROJ

    def token_tile(width):
        return pl.BlockSpec((TM_PROJ, width), lambda b, t: (b * tiles_per_seq + t, 0))

    def param_spec(p):
        if p.ndim == 2:
            return _resident(p.shape, lambda b, t: (0, 0))
        zeros = (0,) * (p.ndim - 1)
        return _resident((None, *p.shape[1:]), lambda b, t: (layer, *zeros))

    qkv_shapes, qkv_specs = [], []
    for window, dil in ATT_GROUPS:
        qkv_shapes.append(jax.ShapeDtypeStruct(
            (3 * N_PAIRS, batch, dil, seq // dil, HEAD_PAIR), BF16))
        qkv_specs.append(pl.BlockSpec(
            (3 * N_PAIRS, None, dil, TM_PROJ // dil, HEAD_PAIR),
            lambda b, t: (0, b, 0, t, 0)))
    return pl.pallas_call(
        _mixer_in_kernel,
        out_shape=(*qkv_shapes, jax.ShapeDtypeStruct((n, D_LRU), BF16)),
        grid=(batch, tiles_per_seq),
        in_specs=[token_tile(D_MODEL)] + [param_spec(p) for p in params],
        out_specs=(*qkv_specs, token_tile(D_LRU)),
        scratch_shapes=[
            pltpu.VMEM((3 * N_PAIRS, TM_PROJ, HEAD_PAIR), F32),
            pltpu.VMEM((3 * N_PAIRS, MAX_DIL * CLS16_PITCH, HEAD_PAIR), F32),
            pltpu.VMEM((N_LRU_LANE_TILES, SUBLANES * LRU_PITCH, LANES), F32),
            pltpu.VMEM((N_LRU_LANE_TILES, (CONV_WIDTH - 1) * SUBLANES, LANES), F32),
            pltpu.VMEM((N_LRU_LANE_TILES, SUBLANES, LANES), F32),
        ],
        compiler_params=pltpu.CompilerParams(
            dimension_semantics=("arbitrary", "arbitrary"), vmem_limit_bytes=VMEM_LIMIT),
        name="mixer_in",
    )(x, *params)


def _attn_kernel(q0_ref, k0_ref, v0_ref, q1_ref, k1_ref, v1_ref,
                 q2_ref, k2_ref, v2_ref, bias_ref, out_ref, so_ref, sm_ref, sl_ref):
    qkv = ((q0_ref, k0_ref, v0_ref), (q1_ref, k1_ref, v1_ref), (q2_ref, k2_ref, v2_ref))
    seq = out_ref.shape[0]
    n_blocks = seq // ATT_BLOCK
    dense_pitch = seq // MAX_DIL + SUBLANES
    lane = lax.broadcasted_iota(jnp.int32, (ATT_BLOCK, HEAD_PAIR), 1)
    head0 = lane < HEAD_DIM

    def block(g, idx, blocks_per_class):
        q_ref, k_ref, v_ref = qkv[g]
        first = (idx & (blocks_per_class - 1)) == 0
        q_start = pl.multiple_of(idx * ATT_BLOCK, ATT_BLOCK)
        k_start = pl.multiple_of(jnp.where(first, idx, idx - 1) * ATT_BLOCK, ATT_BLOCK)
        table = jnp.where(first, 1, 0)
        q = q_ref[pl.ds(q_start, ATT_BLOCK), :]
        k = k_ref[pl.ds(k_start, 2 * ATT_BLOCK), :]
        v = v_ref[pl.ds(k_start, 2 * ATT_BLOCK), :]
        zero = jnp.zeros_like(q)
        q2 = jnp.concatenate([jnp.where(head0, q, zero), jnp.where(head0, zero, q)], axis=0)
        s2 = lax.dot_general(q2, k, (((1,), (1,)), ((), ())), preferred_element_type=F32)
        es, ms, ls = [], [], []
        for hh in range(2):
            s = s2[hh * ATT_BLOCK:(hh + 1) * ATT_BLOCK] + bias_ref[g, hh, table]
            m = jnp.max(s, axis=-1, keepdims=True)
            e = jnp.exp(s - m)
            ls.append(jnp.sum(e, axis=-1, keepdims=True))
            ms.append(m)
            es.append(e.astype(BF16))
        o2 = jnp.dot(jnp.concatenate(es, axis=0), v, preferred_element_type=F32)
        stats = (jnp.where(head0, o2[:ATT_BLOCK], o2[ATT_BLOCK:]),
                 jnp.where(head0, ms[0], ms[1]),
                 jnp.where(head0, ls[0], ls[1]))
        for ref, val in zip((so_ref, sm_ref, sl_ref), stats):
            if ATT_GROUPS[g][1] > 1:
                ref[g, pl.ds(q_start, ATT_BLOCK), :] = val
                continue
            for u in range(ATT_BLOCK // SUBLANES):
                start = (u % 2) * SUBLANES * dense_pitch + idx * (ATT_BLOCK // MAX_DIL) + u // 2
                ref[g, pl.ds(start, SUBLANES, stride=dense_pitch), :] = (
                    val[u * SUBLANES:(u + 1) * SUBLANES])

    for g, (window, dil) in enumerate(ATT_GROUPS):
        blocks_per_class = n_blocks // dil

        def body(it, carry, g=g, blocks_per_class=blocks_per_class):
            for u in range(ATT_UNROLL):
                block(g, it * ATT_UNROLL + u, blocks_per_class)
            return carry

        lax.fori_loop(0, n_blocks // ATT_UNROLL, body, 0)

    rows = seq // MAX_DIL
    for c16 in range(MAX_DIL):
        idxs = []
        for g, (window, dil) in enumerate(ATT_GROUPS):
            if dil == 1:
                idxs.append(pl.ds(c16 * dense_pitch, rows))
            elif dil == MAX_DIL:
                idxs.append(pl.ds(c16 * rows, rows))
            else:
                idxs.append(pl.ds((c16 % dil) * (seq // dil) + c16 // dil, rows,
                                  stride=MAX_DIL // dil))
        ms_ = [sm_ref[g, idx, :] for g, idx in enumerate(idxs)]
        m = jnp.maximum(jnp.maximum(ms_[0], ms_[1]), ms_[2])
        ws = [jnp.exp(mg - m) for mg in ms_]
        num = sum(w * so_ref[g, idx, :] for g, (w, idx) in enumerate(zip(ws, idxs)))
        den = sum(w * sl_ref[g, idx, :] for g, (w, idx) in enumerate(zip(ws, idxs)))
        out_ref[pl.ds(c16, rows, stride=MAX_DIL), :] = num / den


def _attention(qkv_groups, bias, batch, seq):
    in_specs, args = [], []
    for g, (window, dil) in enumerate(ATT_GROUPS):
        arr = qkv_groups[g].reshape(3 * N_PAIRS, batch, seq, HEAD_PAIR)
        for part in range(3):
            in_specs.append(pl.BlockSpec(
                (None, None, seq, HEAD_PAIR),
                lambda b, p, part=part: (part * N_PAIRS + p, b, 0, 0)))
            args.append(arr)
    in_specs.append(pl.BlockSpec(
        (N_GROUPS, 2, 2, ATT_BLOCK, 2 * ATT_BLOCK), lambda b, p: (0, p, 0, 0, 0)))
    args.append(bias)
    scratch = pltpu.VMEM((N_GROUPS, seq + MAX_DIL * SUBLANES, HEAD_PAIR), F32)
    out = pl.pallas_call(
        _attn_kernel,
        out_shape=jax.ShapeDtypeStruct((N_PAIRS, batch, seq, HEAD_PAIR), F32),
        grid=(batch, N_PAIRS),
        in_specs=in_specs,
        out_specs=pl.BlockSpec((None, None, seq, HEAD_PAIR), lambda b, p: (p, b, 0, 0)),
        scratch_shapes=[scratch, scratch, scratch],
        compiler_params=pltpu.CompilerParams(
            dimension_semantics=("arbitrary", "arbitrary"), vmem_limit_bytes=VMEM_LIMIT),
        name="dilated_attention",
    )(*args)
    return out.reshape(N_PAIRS, batch * seq, HEAD_PAIR)


def _attention_bias():
    n = ATT_BLOCK
    heads = jnp.arange(1, N_QKV_HEADS + 1, dtype=F32)
    slopes = jnp.exp2(-ALIBI_MAX_BIAS * heads / N_QKV_HEADS).reshape(N_GROUPS, HEADS_PER_GROUP)
    qi = jnp.arange(n)[:, None]
    ki = jnp.arange(2 * n)[None, :]
    tables = []
    for g, (window, dil) in enumerate(ATT_GROUPS):
        steps = n + qi - ki
        valid = (steps >= 0) & (steps <= n)
        steps_f = qi - ki
        valid_f = steps_f >= 0
        per_head = []
        for hh in range(HEADS_PER_GROUP):
            sl = slopes[g, hh]
            reg = jnp.where(valid, -sl * (dil * steps).astype(F32), NEG_INF)
            fst = jnp.where(valid_f, -sl * (dil * steps_f).astype(F32), NEG_INF)
            per_head.append(jnp.stack([reg, fst]))
        tables.append(jnp.stack(per_head))
    return jnp.stack(tables).astype(F32)


def _block_diag(w):
    depth = w.shape[0]
    per = LRU_GATE_BLK // LRU_BLOCK
    w = w.reshape(depth, D_LRU // LRU_GATE_BLK, per, LRU_BLOCK, LRU_BLOCK)
    eye = jnp.eye(per, dtype=w.dtype)
    full = w[:, :, :, :, None, :] * eye[None, None, :, None, :, None]
    return full.reshape(depth, D_LRU // LRU_GATE_BLK, LRU_GATE_BLK, LRU_GATE_BLK)


def kernel(x, ffn1_norm, ffn1_w_gate, ffn1_w_up, ffn1_w_down, mix_norm, w_in, q_norm, k_norm, conv_w, conv_b, gate_a_w, gate_a_b, gate_x_w, gate_x_b, lru_lambda, w_out, ffn2_norm, ffn2_w_gate, ffn2_w_up, ffn2_w_down):
    batch, seq, _ = x.shape
    depth = w_in.shape[0]
    n = batch * seq

    w_in_b = w_in.astype(BF16)
    w_gates = (0.5 * jnp.concatenate([_block_diag(gate_a_w), _block_diag(gate_x_w)],
                                     axis=-1)).astype(BF16)
    gate_a_b = 0.5 * gate_a_b
    gate_x_b = 0.5 * gate_x_b
    head_id = jnp.arange(MXU_DIM) // HEAD_DIM
    head_mean = ((head_id[:, None] == head_id[None, :]).astype(F32) / HEAD_DIM).astype(BF16)
    score_scale = 1.0 / math.sqrt(HEAD_DIM)
    q_gain = (jnp.tile(q_norm, (1, N_QKV_HEADS)) * score_scale).reshape(depth, 1, D_ATT_QKV)
    k_gain = jnp.tile(k_norm, (1, N_QKV_HEADS)).reshape(depth, 1, D_ATT_QKV)
    bias = _attention_bias()

    def rows(v):
        return v.reshape(depth, 1, v.shape[-1])

    mixer_params = (rows(mix_norm), w_in_b, head_mean, q_gain, k_gain, conv_w, rows(conv_b),
                    w_gates, rows(gate_a_b), rows(gate_x_b), rows(lru_lambda))
    xf = x.reshape(n, D_MODEL)
    for l in range(depth):
        xf = _ffn(xf, l, ffn1_norm, ffn1_w_gate, ffn1_w_up, ffn1_w_down)
        qkv0, qkv1, qkv2, y = _mixer_in(xf, l, mixer_params, batch, seq)
        att = _attention((qkv0, qkv1, qkv2), bias, batch, seq)
        xf = _ffn(xf, l, ffn2_norm, ffn2_w_gate, ffn2_w_up, ffn2_w_down,
                  mixer_out=(att, y, w_out))
    return xf.reshape(batch, seq, D_MODEL)
```

```python
import functools
import math

import jax
import jax.numpy as jnp
from jax import lax
from jax.experimental import pallas as pl
from jax.experimental.pallas import tpu as pltpu

F32 = jnp.float32
BF16 = jnp.bfloat16

D_MODEL = 1024
HEAD_DIM = 64
ATT_GROUPS = ((128, 1), (512, 4), (2048, 16))
N_GROUPS = len(ATT_GROUPS)
HEADS_PER_GROUP = 4
N_QKV_HEADS = N_GROUPS * HEADS_PER_GROUP
D_ATT_QKV = N_QKV_HEADS * HEAD_DIM
D_ATT_OUT = HEADS_PER_GROUP * HEAD_DIM
ALIBI_MAX_BIAS = 8.0
D_LRU = 768
LRU_BLOCK = 64
CONV_WIDTH = 4
LRU_C = 8.0
D_FF = 2816
FFN_RES_WEIGHT = 0.5
RMS_EPS = 1e-6
NEG_INF = -1e30

LANES = 128
SUBLANES = 8
MXU_DIM = 256
VMEM_LIMIT = 56 * 1024 * 1024

FF_CHUNK = MXU_DIM
N_FF_CHUNKS = D_FF // FF_CHUNK
TM_FFN = 512
TM_PROJ = 1024
ATT_BLOCK = 128
HEAD_PAIR = LANES
N_PAIRS = D_ATT_OUT // HEAD_PAIR
MAX_DIL = ATT_GROUPS[-1][1]
ATT_UNROLL = 16
HEADS_PER_MXU = MXU_DIM // HEAD_DIM
LRU_GATE_BLK = MXU_DIM
N_LRU_LANE_TILES = D_LRU // LANES
LRU_SEG = TM_PROJ // SUBLANES
LRU_PITCH = LRU_SEG + SUBLANES
CLS16_PITCH = TM_PROJ // MAX_DIL + SUBLANES


def _resident(block_shape, index_map):
    return pl.BlockSpec(block_shape, index_map, pipeline_mode=pl.Buffered(1))


def _rms_norm(x, gain):
    ms = jnp.mean(x * x, axis=-1, keepdims=True)
    return x * lax.rsqrt(ms + RMS_EPS) * gain


def _ffn_kernel(*refs, with_mixer_out):
    if with_mixer_out:
        x_ref, att_ref, y_ref, wo_ref, gain_ref, wg_ref, wu_ref, wd_ref, o_ref, acc_ref = refs
        mix = jnp.concatenate(
            [att_ref[p].astype(BF16) for p in range(N_PAIRS)] + [y_ref[...]], axis=-1)
        x = x_ref[...] + jnp.dot(mix, wo_ref[...].astype(BF16),
                                 preferred_element_type=F32)
    else:
        x_ref, gain_ref, wg_ref, wu_ref, wd_ref, o_ref, acc_ref = refs
        x = x_ref[...]
    h = _rms_norm(x, gain_ref[...]).astype(BF16)
    for j in range(N_FF_CHUNKS):
        cols = slice(j * FF_CHUNK, (j + 1) * FF_CHUNK)
        g = jnp.dot(h, wg_ref[:, cols].astype(BF16), preferred_element_type=F32)
        u = jnp.dot(h, wu_ref[:, cols].astype(BF16), preferred_element_type=F32)
        a = (g / (1.0 + jnp.exp(-g)) * u).astype(BF16)
        d = jnp.dot(a, wd_ref[cols, :].astype(BF16), preferred_element_type=F32)
        if j == 0:
            acc_ref[...] = d
        else:
            acc_ref[...] += d
    o_ref[...] = x + FFN_RES_WEIGHT * acc_ref[...]


def _ffn(x, layer, gain, wg, wu, wd, mixer_out=None):
    n = x.shape[0]
    row_tile = pl.BlockSpec((TM_FFN, D_MODEL), lambda i: (i, 0))

    def layer_slice(rows, cols):
        return _resident((None, rows, cols), lambda i: (layer, 0, 0))

    in_specs, args = [row_tile], [x]
    if mixer_out is not None:
        att, y, w_out = mixer_out
        in_specs += [
            pl.BlockSpec((N_PAIRS, TM_FFN, HEAD_PAIR), lambda i: (0, i, 0)),
            pl.BlockSpec((TM_FFN, D_LRU), lambda i: (i, 0)),
            layer_slice(D_ATT_OUT + D_LRU, D_MODEL),
        ]
        args += [att, y, w_out]
    in_specs += [
        layer_slice(1, D_MODEL),
        layer_slice(D_MODEL, D_FF),
        layer_slice(D_MODEL, D_FF),
        layer_slice(D_FF, D_MODEL),
    ]
    args += [gain.reshape(gain.shape[0], 1, D_MODEL), wg, wu, wd]
    return pl.pallas_call(
        functools.partial(_ffn_kernel, with_mixer_out=mixer_out is not None),
        out_shape=jax.ShapeDtypeStruct((n, D_MODEL), F32),
        grid=(n // TM_FFN,),
        in_specs=in_specs,
        out_specs=row_tile,
        scratch_shapes=[pltpu.VMEM((TM_FFN, D_MODEL), F32)],
        compiler_params=pltpu.CompilerParams(
            dimension_semantics=("arbitrary",), vmem_limit_bytes=VMEM_LIMIT),
        name="ffn_mixer_out" if mixer_out is not None else "ffn",
    )(*args)


def _mixer_in_kernel(x_ref, gain_ref, w_ref, hm_ref, qg_ref, kg_ref,
                     cw_ref, cb_ref, wgate_ref, ba_ref, bx_ref, lam_ref,
                     o0_ref, o1_ref, o2_ref, y_ref,
                     slab_ref, slab16_ref, seg_ref, hist_ref, h_ref):
    @pl.when(pl.program_id(1) == 0)
    def _():
        hist_ref[...] = jnp.zeros(hist_ref.shape, F32)
        h_ref[...] = jnp.zeros(h_ref.shape, F32)

    x = x_ref[...]
    h = _rms_norm(x, gain_ref[...]).astype(BF16)

    def head_norm(t, gain):
        sq = (t * t).astype(BF16)
        ms = jnp.concatenate(
            [jnp.dot(sq[:, i * MXU_DIM:(i + 1) * MXU_DIM], hm_ref[...],
                     preferred_element_type=F32)
             for i in range(D_ATT_QKV // MXU_DIM)], axis=-1)
        return t * lax.rsqrt(ms + RMS_EPS) * gain

    c0 = 3 * D_ATT_QKV
    lru_x = jnp.dot(h, w_ref[:, c0:c0 + D_LRU], preferred_element_type=F32)
    q = jnp.dot(h, w_ref[:, 0:D_ATT_QKV], preferred_element_type=F32)

    sub = lax.broadcasted_iota(jnp.int32, (SUBLANES, LANES), 0)
    seg_rows = [slice(i * SUBLANES, (i + 1) * SUBLANES) for i in range(LRU_SEG)]

    def to_segment_order(col, lt):
        for s in range(SUBLANES):
            seg_ref[lt, pl.ds(s * LRU_PITCH, LRU_SEG), :] = col[s * LRU_SEG:(s + 1) * LRU_SEG]
        return [seg_ref[lt, pl.ds(i, SUBLANES, stride=LRU_PITCH), :] for i in range(LRU_SEG)]

    def to_time_order(vregs, lt):
        for i, vreg in enumerate(vregs):
            seg_ref[lt, pl.ds(i, SUBLANES, stride=LRU_PITCH), :] = vreg
        return jnp.concatenate(
            [seg_ref[lt, pl.ds(s * LRU_PITCH, LRU_SEG), :] for s in range(SUBLANES)], axis=0)

    n_hist = CONV_WIDTH - 1
    xc_cols = []
    for lt in range(N_LRU_LANE_TILES):
        lanes = slice(lt * LANES, (lt + 1) * LANES)
        xv = to_segment_order(lru_x[:, lanes], lt)
        head = []
        for j in range(n_hist):
            last = xv[LRU_SEG - n_hist + j]
            prev = hist_ref[lt, seg_rows[j], :]
            head.append(jnp.where(sub == 0, pltpu.roll(prev, 1, 0), pltpu.roll(last, 1, 0)))
            hist_ref[lt, seg_rows[j], :] = last
        ext = jnp.concatenate(head + xv, axis=0)
        xc_l = cb_ref[:, lanes]
        for w in range(CONV_WIDTH):
            xc_l = xc_l + cw_ref[w:w + 1, lanes] * ext[w * SUBLANES:w * SUBLANES + TM_PROJ]
        xc_cols.append(xc_l)
    xc = jnp.concatenate(xc_cols, axis=-1)

    xcb = xc.astype(BF16)
    gates = [jnp.dot(xcb[:, i * LRU_GATE_BLK:(i + 1) * LRU_GATE_BLK], wgate_ref[i],
                     preferred_element_type=F32)
             for i in range(D_LRU // LRU_GATE_BLK)]
    k = jnp.dot(h, w_ref[:, D_ATT_QKV:2 * D_ATT_QKV], preferred_element_type=F32)
    gate = jnp.dot(h, w_ref[:, c0 + D_LRU:c0 + 2 * D_LRU], preferred_element_type=F32)
    v = jnp.dot(h, w_ref[:, 2 * D_ATT_QKV:3 * D_ATT_QKV], preferred_element_type=F32)
    tanh_a = jnp.tanh(jnp.concatenate([t[:, :LRU_GATE_BLK] for t in gates], axis=-1)
                      + ba_ref[...])
    tanh_x = jnp.tanh(jnp.concatenate([t[:, LRU_GATE_BLK:] for t in gates], axis=-1)
                      + bx_ref[...])
    z = -lam_ref[...]
    softplus = jnp.maximum(z, 0.0) + jnp.log1p(jnp.exp(-jnp.abs(z)))
    half_decay = (-0.5 * LRU_C * math.log2(math.e)) * softplus
    a = jnp.exp2(half_decay + half_decay * tanh_a)
    half_xc = 0.5 * xc
    s = 1.0 - a * a
    root = jnp.where(s > 0.0, s * lax.rsqrt(s), 0.0)
    b = root * (half_xc + half_xc * tanh_x)

    h_cols = []
    for lt in range(N_LRU_LANE_TILES):
        lanes = slice(lt * LANES, (lt + 1) * LANES)
        a_l, b_l = a[:, lanes], b[:, lanes]
        h_loc = jnp.zeros((SUBLANES, LANES), F32)
        a_cum = jnp.ones((SUBLANES, LANES), F32)
        h_locs, a_cums = [], []
        for rows in seg_rows:
            h_loc = a_l[rows] * h_loc + b_l[rows]
            a_cum = a_cum * a_l[rows]
            h_locs.append(h_loc)
            a_cums.append(a_cum)
        carry = h_ref[lt, 0:1, :]
        carry_in = jnp.zeros((SUBLANES, LANES), F32)
        for s in range(SUBLANES):
            carry_in = jnp.where(sub == s, carry, carry_in)
            carry = a_cum[s:s + 1, :] * carry + h_loc[s:s + 1, :]
        h_ref[lt, 0:1, :] = carry
        h_cols.append(to_time_order(
            [hl + ac * carry_in for hl, ac in zip(h_locs, a_cums)], lt))
    h_all = jnp.concatenate(h_cols, axis=-1)

    inner = math.sqrt(2.0 / math.pi) * (gate + 0.044715 * (gate * gate * gate))
    gelu = 0.5 * gate * (1.0 + jnp.tanh(inner))
    y_ref[...] = (h_all * gelu).astype(y_ref.dtype)

    qn = head_norm(q, qg_ref[...])
    kn = head_norm(k, kg_ref[...])
    outs = (o0_ref, o1_ref, o2_ref)
    for g, (window, dil) in enumerate(ATT_GROUPS):
        for part, val in enumerate((qn, kn, v)):
            for p in range(N_PAIRS):
                c = g * D_ATT_OUT + p * HEAD_PAIR
                piece = val[:, c:c + HEAD_PAIR]
                dst = outs[g].at[part * N_PAIRS + p]
                slab = part * N_PAIRS + p
                if dil == 1:
                    dst[0] = piece.astype(BF16)
                elif dil == SUBLANES * 2:
                    per_class = TM_PROJ // dil
                    for i in range(TM_PROJ // SUBLANES):
                        start = (i % 2) * SUBLANES * CLS16_PITCH + i // 2
                        slab16_ref[slab, pl.ds(start, SUBLANES, stride=CLS16_PITCH), :] = (
                            piece[i * SUBLANES:(i + 1) * SUBLANES])
                    for cls in range(dil):
                        dst[cls] = slab16_ref[
                            slab, pl.ds(cls * CLS16_PITCH, per_class), :].astype(BF16)
                else:
                    slab_ref[slab] = piece
                    for cls in range(dil):
                        rows = pl.ds(cls, TM_PROJ // dil, stride=dil)
                        dst[cls] = slab_ref[slab, rows, :].astype(BF16)


def _mixer_in(x, layer, params, batch, seq):
    n = x.shape[0]
    tiles_per_seq = seq // TM_PROJ

    def token_tile(width):
        return pl.BlockSpec((TM_PROJ, width), lambda b, t: (b * tiles_per_seq + t, 0))

    def param_spec(p):
        if p.ndim == 2:
            return _resident(p.shape, lambda b, t: (0, 0))
        zeros = (0,) * (p.ndim - 1)
        return _resident((None, *p.shape[1:]), lambda b, t: (layer, *zeros))

    qkv_shapes, qkv_specs = [], []
    for window, dil in ATT_GROUPS:
        qkv_shapes.append(jax.ShapeDtypeStruct(
            (3 * N_PAIRS, batch, dil, seq // dil, HEAD_PAIR), BF16))
        qkv_specs.append(pl.BlockSpec(
            (3 * N_PAIRS, None, dil, TM_PROJ // dil, HEAD_PAIR),
            lambda b, t: (0, b, 0, t, 0)))
    return pl.pallas_call(
        _mixer_in_kernel,
        out_shape=(*qkv_shapes, jax.ShapeDtypeStruct((n, D_LRU), BF16)),
        grid=(batch, tiles_per_seq),
        in_specs=[token_tile(D_MODEL)] + [param_spec(p) for p in params],
        out_specs=(*qkv_specs, token_tile(D_LRU)),
        scratch_shapes=[
            pltpu.VMEM((3 * N_PAIRS, TM_PROJ, HEAD_PAIR), F32),
            pltpu.VMEM((3 * N_PAIRS, MAX_DIL * CLS16_PITCH, HEAD_PAIR), F32),
            pltpu.VMEM((N_LRU_LANE_TILES, SUBLANES * LRU_PITCH, LANES), F32),
            pltpu.VMEM((N_LRU_LANE_TILES, (CONV_WIDTH - 1) * SUBLANES, LANES), F32),
            pltpu.VMEM((N_LRU_LANE_TILES, SUBLANES, LANES), F32),
        ],
        compiler_params=pltpu.CompilerParams(
            dimension_semantics=("arbitrary", "arbitrary"), vmem_limit_bytes=VMEM_LIMIT),
        name="mixer_in",
    )(x, *params)


def _attn_kernel(q0_ref, k0_ref, v0_ref, q1_ref, k1_ref, v1_ref,
                 q2_ref, k2_ref, v2_ref, bias_ref, out_ref, so_ref, sm_ref, sl_ref):
    qkv = ((q0_ref, k0_ref, v0_ref), (q1_ref, k1_ref, v1_ref), (q2_ref, k2_ref, v2_ref))
    seq = out_ref.shape[0]
    n_blocks = seq // ATT_BLOCK
    dense_pitch = seq // MAX_DIL + SUBLANES
    lane = lax.broadcasted_iota(jnp.int32, (ATT_BLOCK, HEAD_PAIR), 1)
    head0 = lane < HEAD_DIM

    def block(g, idx, blocks_per_class):
        q_ref, k_ref, v_ref = qkv[g]
        first = (idx & (blocks_per_class - 1)) == 0
        q_start = pl.multiple_of(idx * ATT_BLOCK, ATT_BLOCK)
        k_start = pl.multiple_of(jnp.where(first, idx, idx - 1) * ATT_BLOCK, ATT_BLOCK)
        table = jnp.where(first, 1, 0)
        q = q_ref[pl.ds(q_start, ATT_BLOCK), :]
        k = k_ref[pl.ds(k_start, 2 * ATT_BLOCK), :]
        v = v_ref[pl.ds(k_start, 2 * ATT_BLOCK), :]
        zero = jnp.zeros_like(q)
        q2 = jnp.concatenate([jnp.where(head0, q, zero), jnp.where(head0, zero, q)], axis=0)
        s2 = lax.dot_general(q2, k, (((1,), (1,)), ((), ())), preferred_element_type=F32)
        es, ms, ls = [], [], []
        for hh in range(2):
            s = s2[hh * ATT_BLOCK:(hh + 1) * ATT_BLOCK] + bias_ref[g, hh, table]
            m = jnp.max(s, axis=-1, keepdims=True)
            e = jnp.exp(s - m)
            ls.append(jnp.sum(e, axis=-1, keepdims=True))
            ms.append(m)
            es.append(e.astype(BF16))
        o2 = jnp.dot(jnp.concatenate(es, axis=0), v, preferred_element_type=F32)
        stats = (jnp.where(head0, o2[:ATT_BLOCK], o2[ATT_BLOCK:]),
                 jnp.where(head0, ms[0], ms[1]),
                 jnp.where(head0, ls[0], ls[1]))
        for ref, val in zip((so_ref, sm_ref, sl_ref), stats):
            if ATT_GROUPS[g][1] > 1:
                ref[g, pl.ds(q_start, ATT_BLOCK), :] = val
                continue
            for u in range(ATT_BLOCK // SUBLANES):
                start = (u % 2) * SUBLANES * dense_pitch + idx * (ATT_BLOCK // MAX_DIL) + u // 2
                ref[g, pl.ds(start, SUBLANES, stride=dense_pitch), :] = (
                    val[u * SUBLANES:(u + 1) * SUBLANES])

    for g, (window, dil) in enumerate(ATT_GROUPS):
        blocks_per_class = n_blocks // dil

        def body(it, carry, g=g, blocks_per_class=blocks_per_class):
            for u in range(ATT_UNROLL):
                block(g, it * ATT_UNROLL + u, blocks_per_class)
            return carry

        lax.fori_loop(0, n_blocks // ATT_UNROLL, body, 0)

    rows = seq // MAX_DIL
    for c16 in range(MAX_DIL):
        idxs = []
        for g, (window, dil) in enumerate(ATT_GROUPS):
            if dil == 1:
                idxs.append(pl.ds(c16 * dense_pitch, rows))
            elif dil == MAX_DIL:
                idxs.append(pl.ds(c16 * rows, rows))
            else:
                idxs.append(pl.ds((c16 % dil) * (seq // dil) + c16 // dil, rows,
                                  stride=MAX_DIL // dil))
        ms_ = [sm_ref[g, idx, :] for g, idx in enumerate(idxs)]
        m = jnp.maximum(jnp.maximum(ms_[0], ms_[1]), ms_[2])
        ws = [jnp.exp(mg - m) for mg in ms_]
        num = sum(w * so_ref[g, idx, :] for g, (w, idx) in enumerate(zip(ws, idxs)))
        den = sum(w * sl_ref[g, idx, :] for g, (w, idx) in enumerate(zip(ws, idxs)))
        out_ref[pl.ds(c16, rows, stride=MAX_DIL), :] = num / den


def _attention(qkv_groups, bias, batch, seq):
    in_specs, args = [], []
    for g, (window, dil) in enumerate(ATT_GROUPS):
        arr = qkv_groups[g].reshape(3 * N_PAIRS, batch, seq, HEAD_PAIR)
        for part in range(3):
            in_specs.append(pl.BlockSpec(
                (None, None, seq, HEAD_PAIR),
                lambda b, p, part=part: (part * N_PAIRS + p, b, 0, 0)))
            args.append(arr)
    in_specs.append(pl.BlockSpec(
        (N_GROUPS, 2, 2, ATT_BLOCK, 2 * ATT_BLOCK), lambda b, p: (0, p, 0, 0, 0)))
    args.append(bias)
    scratch = pltpu.VMEM((N_GROUPS, seq + MAX_DIL * SUBLANES, HEAD_PAIR), F32)
    out = pl.pallas_call(
        _attn_kernel,
        out_shape=jax.ShapeDtypeStruct((N_PAIRS, batch, seq, HEAD_PAIR), F32),
        grid=(batch, N_PAIRS),
        in_specs=in_specs,
        out_specs=pl.BlockSpec((None, None, seq, HEAD_PAIR), lambda b, p: (p, b, 0, 0)),
        scratch_shapes=[scratch, scratch, scratch],
        compiler_params=pltpu.CompilerParams(
            dimension_semantics=("arbitrary", "arbitrary"), vmem_limit_bytes=VMEM_LIMIT),
        name="dilated_attention",
    )(*args)
    return out.reshape(N_PAIRS, batch * seq, HEAD_PAIR)


def _attention_bias():
    n = ATT_BLOCK
    heads = jnp.arange(1, N_QKV_HEADS + 1, dtype=F32)
    slopes = jnp.exp2(-ALIBI_MAX_BIAS * heads / N_QKV_HEADS).reshape(N_GROUPS, HEADS_PER_GROUP)
    qi = jnp.arange(n)[:, None]
    ki = jnp.arange(2 * n)[None, :]
    tables = []
    for g, (window, dil) in enumerate(ATT_GROUPS):
        steps = n + qi - ki
        valid = (steps >= 0) & (steps <= n)
        steps_f = qi - ki
        valid_f = steps_f >= 0
        per_head = []
        for hh in range(HEADS_PER_GROUP):
            sl = slopes[g, hh]
            reg = jnp.where(valid, -sl * (dil * steps).astype(F32), NEG_INF)
            fst = jnp.where(valid_f, -sl * (dil * steps_f).astype(F32), NEG_INF)
            per_head.append(jnp.stack([reg, fst]))
        tables.append(jnp.stack(per_head))
    return jnp.stack(tables).astype(F32)


def _block_diag(w):
    depth = w.shape[0]
    per = LRU_GATE_BLK // LRU_BLOCK
    w = w.reshape(depth, D_LRU // LRU_GATE_BLK, per, LRU_BLOCK, LRU_BLOCK)
    eye = jnp.eye(per, dtype=w.dtype)
    full = w[:, :, :, :, None, :] * eye[None, None, :, None, :, None]
    return full.reshape(depth, D_LRU // LRU_GATE_BLK, LRU_GATE_BLK, LRU_GATE_BLK)


def kernel(x, ffn1_norm, ffn1_w_gate, ffn1_w_up, ffn1_w_down, mix_norm, w_in, q_norm, k_norm, conv_w, conv_b, gate_a_w, gate_a_b, gate_x_w, gate_x_b, lru_lambda, w_out, ffn2_norm, ffn2_w_gate, ffn2_w_up, ffn2_w_down):
    batch, seq, _ = x.shape
    depth = w_in.shape[0]
    n = batch * seq

    w_in_b = w_in.astype(BF16)
    w_gates = (0.5 * jnp.concatenate([_block_diag(gate_a_w), _block_diag(gate_x_w)],
                                     axis=-1)).astype(BF16)
    gate_a_b = 0.5 * gate_a_b
    gate_x_b = 0.5 * gate_x_b
    head_id = jnp.arange(MXU_DIM) // HEAD_DIM
    head_mean = ((head_id[:, None] == head_id[None, :]).astype(F32) / HEAD_DIM).astype(BF16)
    score_scale = 1.0 / math.sqrt(HEAD_DIM)
    q_gain = (jnp.tile(q_norm, (1, N_QKV_HEADS)) * score_scale).reshape(depth, 1, D_ATT_QKV)
    k_gain = jnp.tile(k_norm, (1, N_QKV_HEADS)).reshape(depth, 1, D_ATT_QKV)
    bias = _attention_bias()

    def rows(v):
        return v.reshape(depth, 1, v.shape[-1])

    mixer_params = (rows(mix_norm), w_in_b, head_mean, q_gain, k_gain, conv_w, rows(conv_b),
                    w_gates, rows(gate_a_b), rows(gate_x_b), rows(lru_lambda))
    xf = x.reshape(n, D_MODEL)
    for l in range(depth):
        xf = _ffn(xf, l, ffn1_norm, ffn1_w_gate, ffn1_w_up, ffn1_w_down)
        qkv0, qkv1, qkv2, y = _mixer_in(xf, l, mixer_params, batch, seq)
        att = _attention((qkv0, qkv1, qkv2), bias, batch, seq)
        xf = _ffn(xf, l, ffn2_norm, ffn2_w_gate, ffn2_w_up, ffn2_w_down,
                  mixer_out=(att, y, w_out))
    return xf.reshape(batch, seq, D_MODEL)
```

```python
import functools
import math

import jax
import jax.numpy as jnp
from jax import lax
from jax.experimental import pallas as pl
from jax.experimental.pallas import tpu as pltpu

F32 = jnp.float32
BF16 = jnp.bfloat16

D_MODEL = 1024
HEAD_DIM = 64
ATT_GROUPS = ((128, 1), (512, 4), (2048, 16))
N_GROUPS = len(ATT_GROUPS)
HEADS_PER_GROUP = 4
N_QKV_HEADS = N_GROUPS * HEADS_PER_GROUP
D_ATT_QKV = N_QKV_HEADS * HEAD_DIM
D_ATT_OUT = HEADS_PER_GROUP * HEAD_DIM
ALIBI_MAX_BIAS = 8.0
D_LRU = 768
LRU_BLOCK = 64
CONV_WIDTH = 4
LRU_C = 8.0
D_FF = 2816
FFN_RES_WEIGHT = 0.5
RMS_EPS = 1e-6
NEG_INF = -1e30

LANES = 128
SUBLANES = 8
MXU_DIM = 256
VMEM_LIMIT = 56 * 1024 * 1024

FF_CHUNK = MXU_DIM
N_FF_CHUNKS = D_FF // FF_CHUNK
TM_FFN = 1024
TM_FFN_MIX = 512
TM_PROJ = 1024
ATT_BLOCK = 128
HEAD_PAIR = LANES
N_PAIRS = D_ATT_OUT // HEAD_PAIR
MAX_DIL = ATT_GROUPS[-1][1]
ATT_UNROLL = 16
HEADS_PER_MXU = MXU_DIM // HEAD_DIM
LRU_GATE_BLK = MXU_DIM
N_LRU_LANE_TILES = D_LRU // LANES
LRU_SEG = TM_PROJ // SUBLANES
LRU_PITCH = LRU_SEG + SUBLANES
CLS16_PITCH = TM_PROJ // MAX_DIL + SUBLANES


def _resident(block_shape, index_map):
    return pl.BlockSpec(block_shape, index_map, pipeline_mode=pl.Buffered(1))


def _rms_norm(x, gain):
    ms = jnp.mean(x * x, axis=-1, keepdims=True)
    return x * lax.rsqrt(ms + RMS_EPS) * gain


def _ffn_kernel(*refs, with_mixer_out):
    if with_mixer_out:
        x_ref, att_ref, y_ref, wo_ref, gain_ref, wg_ref, wu_ref, wd_ref, o_ref = refs
        mix = jnp.concatenate(
            [att_ref[p].astype(BF16) for p in range(N_PAIRS)] + [y_ref[...]], axis=-1)
        x = x_ref[...] + jnp.dot(mix, wo_ref[...].astype(BF16),
                                 preferred_element_type=F32)
    else:
        x_ref, gain_ref, wg_ref, wu_ref, wd_ref, o_ref = refs
        x = x_ref[...]
    h = _rms_norm(x, gain_ref[...]).astype(BF16)
    for j in range(N_FF_CHUNKS):
        cols = slice(j * FF_CHUNK, (j + 1) * FF_CHUNK)
        g = jnp.dot(h, wg_ref[:, cols].astype(BF16), preferred_element_type=F32)
        u = jnp.dot(h, wu_ref[:, cols].astype(BF16), preferred_element_type=F32)
        a = (g / (1.0 + jnp.exp(-g)) * u).astype(BF16)
        d = jnp.dot(a, wd_ref[cols, :].astype(BF16), preferred_element_type=F32)
        if j == 0:
            o_ref[...] = d
        else:
            o_ref[...] += d
    o_ref[...] = x + FFN_RES_WEIGHT * o_ref[...]


def _ffn(x, layer, gain, wg, wu, wd, mixer_out=None):
    n = x.shape[0]
    tm = TM_FFN if mixer_out is None else TM_FFN_MIX
    row_tile = pl.BlockSpec((tm, D_MODEL), lambda i: (i, 0))

    def layer_slice(rows, cols):
        return _resident((None, rows, cols), lambda i: (layer, 0, 0))

    in_specs, args = [row_tile], [x]
    if mixer_out is not None:
        att, y, w_out = mixer_out
        in_specs += [
            pl.BlockSpec((N_PAIRS, tm, HEAD_PAIR), lambda i: (0, i, 0)),
            pl.BlockSpec((tm, D_LRU), lambda i: (i, 0)),
            layer_slice(D_ATT_OUT + D_LRU, D_MODEL),
        ]
        args += [att, y, w_out]
    in_specs += [
        layer_slice(1, D_MODEL),
        layer_slice(D_MODEL, D_FF),
        layer_slice(D_MODEL, D_FF),
        layer_slice(D_FF, D_MODEL),
    ]
    args += [gain.reshape(gain.shape[0], 1, D_MODEL), wg, wu, wd]
    return pl.pallas_call(
        functools.partial(_ffn_kernel, with_mixer_out=mixer_out is not None),
        out_shape=jax.ShapeDtypeStruct((n, D_MODEL), F32),
        grid=(n // tm,),
        in_specs=in_specs,
        out_specs=row_tile,
        compiler_params=pltpu.CompilerParams(
            dimension_semantics=("arbitrary",), vmem_limit_bytes=VMEM_LIMIT),
        name="ffn_mixer_out" if mixer_out is not None else "ffn",
    )(*args)


def _mixer_in_kernel(x_ref, gain_ref, w_ref, hm_ref, qg_ref, kg_ref,
                     cw_ref, cb_ref, wgate_ref, ba_ref, bx_ref, lam_ref,
                     o0_ref, o1_ref, o2_ref, y_ref,
                     slab_ref, slab16_ref, seg_ref, hist_ref, h_ref):
    @pl.when(pl.program_id(1) == 0)
    def _():
        hist_ref[...] = jnp.zeros(hist_ref.shape, F32)
        h_ref[...] = jnp.zeros(h_ref.shape, F32)

    x = x_ref[...]
    h = _rms_norm(x, gain_ref[...]).astype(BF16)

    def head_norm(t, gain):
        sq = (t * t).astype(BF16)
        ms = jnp.concatenate(
            [jnp.dot(sq[:, i * MXU_DIM:(i + 1) * MXU_DIM], hm_ref[...],
                     preferred_element_type=F32)
             for i in range(D_ATT_QKV // MXU_DIM)], axis=-1)
        return t * lax.rsqrt(ms + RMS_EPS) * gain

    c0 = 3 * D_ATT_QKV
    lru_x = jnp.dot(h, w_ref[:, c0:c0 + D_LRU], preferred_element_type=F32)
    q = jnp.dot(h, w_ref[:, 0:D_ATT_QKV], preferred_element_type=F32)

    sub = lax.broadcasted_iota(jnp.int32, (SUBLANES, LANES), 0)
    seg_rows = [slice(i * SUBLANES, (i + 1) * SUBLANES) for i in range(LRU_SEG)]

    def to_segment_order(col, lt):
        for s in range(SUBLANES):
            seg_ref[lt, pl.ds(s * LRU_PITCH, LRU_SEG), :] = col[s * LRU_SEG:(s + 1) * LRU_SEG]
        return [seg_ref[lt, pl.ds(i, SUBLANES, stride=LRU_PITCH), :] for i in range(LRU_SEG)]

    def to_time_order(vregs, lt):
        for i, vreg in enumerate(vregs):
            seg_ref[lt, pl.ds(i, SUBLANES, stride=LRU_PITCH), :] = vreg
        return jnp.concatenate(
            [seg_ref[lt, pl.ds(s * LRU_PITCH, LRU_SEG), :] for s in range(SUBLANES)], axis=0)

    n_hist = CONV_WIDTH - 1
    xc_cols = []
    for lt in range(N_LRU_LANE_TILES):
        lanes = slice(lt * LANES, (lt + 1) * LANES)
        xv = to_segment_order(lru_x[:, lanes], lt)
        head = []
        for j in range(n_hist):
            last = xv[LRU_SEG - n_hist + j]
            prev = hist_ref[lt, seg_rows[j], :]
            head.append(jnp.where(sub == 0, pltpu.roll(prev, 1, 0), pltpu.roll(last, 1, 0)))
            hist_ref[lt, seg_rows[j], :] = last
        ext = jnp.concatenate(head + xv, axis=0)
        xc_l = cb_ref[:, lanes]
        for w in range(CONV_WIDTH):
            xc_l = xc_l + cw_ref[w:w + 1, lanes] * ext[w * SUBLANES:w * SUBLANES + TM_PROJ]
        xc_cols.append(xc_l)
    xc = jnp.concatenate(xc_cols, axis=-1)

    xcb = xc.astype(BF16)
    gates = [jnp.dot(xcb[:, i * LRU_GATE_BLK:(i + 1) * LRU_GATE_BLK], wgate_ref[i],
                     preferred_element_type=F32)
             for i in range(D_LRU // LRU_GATE_BLK)]
    k = jnp.dot(h, w_ref[:, D_ATT_QKV:2 * D_ATT_QKV], preferred_element_type=F32)
    gate = jnp.dot(h, w_ref[:, c0 + D_LRU:c0 + 2 * D_LRU], preferred_element_type=F32)
    v = jnp.dot(h, w_ref[:, 2 * D_ATT_QKV:3 * D_ATT_QKV], preferred_element_type=F32)
    tanh_a = jnp.tanh(jnp.concatenate([t[:, :LRU_GATE_BLK] for t in gates], axis=-1)
                      + ba_ref[...])
    tanh_x = jnp.tanh(jnp.concatenate([t[:, LRU_GATE_BLK:] for t in gates], axis=-1)
                      + bx_ref[...])
    z = -lam_ref[...]
    softplus = jnp.maximum(z, 0.0) + jnp.log1p(jnp.exp(-jnp.abs(z)))
    half_decay = (-0.5 * LRU_C * math.log2(math.e)) * softplus
    a = jnp.exp2(half_decay + half_decay * tanh_a)
    half_xc = 0.5 * xc
    s = 1.0 - a * a
    root = jnp.where(s > 0.0, s * lax.rsqrt(s), 0.0)
    b = root * (half_xc + half_xc * tanh_x)

    h_cols = []
    for lt in range(N_LRU_LANE_TILES):
        lanes = slice(lt * LANES, (lt + 1) * LANES)
        a_l, b_l = a[:, lanes], b[:, lanes]
        h_loc = jnp.zeros((SUBLANES, LANES), F32)
        a_cum = jnp.ones((SUBLANES, LANES), F32)
        h_locs, a_cums = [], []
        for rows in seg_rows:
            h_loc = a_l[rows] * h_loc + b_l[rows]
            a_cum = a_cum * a_l[rows]
            h_locs.append(h_loc)
            a_cums.append(a_cum)
        carry = h_ref[lt, 0:1, :]
        carry_in = jnp.zeros((SUBLANES, LANES), F32)
        for s in range(SUBLANES):
            carry_in = jnp.where(sub == s, carry, carry_in)
            carry = a_cum[s:s + 1, :] * carry + h_loc[s:s + 1, :]
        h_ref[lt, 0:1, :] = carry
        h_cols.append(to_time_order(
            [hl + ac * carry_in for hl, ac in zip(h_locs, a_cums)], lt))
    h_all = jnp.concatenate(h_cols, axis=-1)

    inner = math.sqrt(2.0 / math.pi) * (gate + 0.044715 * (gate * gate * gate))
    gelu = 0.5 * gate * (1.0 + jnp.tanh(inner))
    y_ref[...] = (h_all * gelu).astype(y_ref.dtype)

    qn = head_norm(q, qg_ref[...])
    kn = head_norm(k, kg_ref[...])
    outs = (o0_ref, o1_ref, o2_ref)
    for g, (window, dil) in enumerate(ATT_GROUPS):
        for part, val in enumerate((qn, kn, v)):
            for p in range(N_PAIRS):
                c = g * D_ATT_OUT + p * HEAD_PAIR
                piece = val[:, c:c + HEAD_PAIR]
                dst = outs[g].at[part * N_PAIRS + p]
                slab = part * N_PAIRS + p
                if dil == 1:
                    dst[0] = piece.astype(BF16)
                elif dil == SUBLANES * 2:
                    per_class = TM_PROJ // dil
                    for i in range(TM_PROJ // SUBLANES):
                        start = (i % 2) * SUBLANES * CLS16_PITCH + i // 2
                        slab16_ref[slab, pl.ds(start, SUBLANES, stride=CLS16_PITCH), :] = (
                            piece[i * SUBLANES:(i + 1) * SUBLANES])
                    for cls in range(dil):
                        dst[cls] = slab16_ref[
                            slab, pl.ds(cls * CLS16_PITCH, per_class), :].astype(BF16)
                else:
                    slab_ref[slab] = piece
                    for cls in range(dil):
                        rows = pl.ds(cls, TM_PROJ // dil, stride=dil)
                        dst[cls] = slab_ref[slab, rows, :].astype(BF16)


def _mixer_in(x, layer, params, batch, seq):
    n = x.shape[0]
    tiles_per_seq = seq // TM_PROJ

    def token_tile(width):
        return pl.BlockSpec((TM_PROJ, width), lambda b, t: (b * tiles_per_seq + t, 0))

    def param_spec(p):
        if p.ndim == 2:
            return _resident(p.shape, lambda b, t: (0, 0))
        zeros = (0,) * (p.ndim - 1)
        return _resident((None, *p.shape[1:]), lambda b, t: (layer, *zeros))

    qkv_shapes, qkv_specs = [], []
    for window, dil in ATT_GROUPS:
        qkv_shapes.append(jax.ShapeDtypeStruct(
            (3 * N_PAIRS, batch, dil, seq // dil, HEAD_PAIR), BF16))
        qkv_specs.append(pl.BlockSpec(
            (3 * N_PAIRS, None, dil, TM_PROJ // dil, HEAD_PAIR),
            lambda b, t: (0, b, 0, t, 0)))
    return pl.pallas_call(
        _mixer_in_kernel,
        out_shape=(*qkv_shapes, jax.ShapeDtypeStruct((n, D_LRU), BF16)),
        grid=(batch, tiles_per_seq),
        in_specs=[token_tile(D_MODEL)] + [param_spec(p) for p in params],
        out_specs=(*qkv_specs, token_tile(D_LRU)),
        scratch_shapes=[
            pltpu.VMEM((3 * N_PAIRS, TM_PROJ, HEAD_PAIR), F32),
            pltpu.VMEM((3 * N_PAIRS, MAX_DIL * CLS16_PITCH, HEAD_PAIR), F32),
            pltpu.VMEM((N_LRU_LANE_TILES, SUBLANES * LRU_PITCH, LANES), F32),
            pltpu.VMEM((N_LRU_LANE_TILES, (CONV_WIDTH - 1) * SUBLANES, LANES), F32),
            pltpu.VMEM((N_LRU_LANE_TILES, SUBLANES, LANES), F32),
        ],
        compiler_params=pltpu.CompilerParams(
            dimension_semantics=("arbitrary", "arbitrary"), vmem_limit_bytes=VMEM_LIMIT),
        name="mixer_in",
    )(x, *params)


def _attn_kernel(q0_ref, k0_ref, v0_ref, q1_ref, k1_ref, v1_ref,
                 q2_ref, k2_ref, v2_ref, bias_ref, out_ref, so_ref, sm_ref, sl_ref):
    qkv = ((q0_ref, k0_ref, v0_ref), (q1_ref, k1_ref, v1_ref), (q2_ref, k2_ref, v2_ref))
    seq = out_ref.shape[0]
    n_blocks = seq // ATT_BLOCK
    dense_pitch = seq // MAX_DIL + SUBLANES
    lane = lax.broadcasted_iota(jnp.int32, (ATT_BLOCK, HEAD_PAIR), 1)
    head0 = lane < HEAD_DIM

    def block(g, idx, blocks_per_class):
        q_ref, k_ref, v_ref = qkv[g]
        first = (idx & (blocks_per_class - 1)) == 0
        q_start = pl.multiple_of(idx * ATT_BLOCK, ATT_BLOCK)
        k_start = pl.multiple_of(jnp.where(first, idx, idx - 1) * ATT_BLOCK, ATT_BLOCK)
        table = jnp.where(first, 1, 0)
        q = q_ref[pl.ds(q_start, ATT_BLOCK), :]
        k = k_ref[pl.ds(k_start, 2 * ATT_BLOCK), :]
        v = v_ref[pl.ds(k_start, 2 * ATT_BLOCK), :]
        zero = jnp.zeros_like(q)
        q2 = jnp.concatenate([jnp.where(head0, q, zero), jnp.where(head0, zero, q)], axis=0)
        s2 = lax.dot_general(q2, k, (((1,), (1,)), ((), ())), preferred_element_type=F32)
        es, ms, ls = [], [], []
        for hh in range(2):
            s = s2[hh * ATT_BLOCK:(hh + 1) * ATT_BLOCK] + bias_ref[g, hh, table]
            m = jnp.max(s, axis=-1, keepdims=True)
            e = jnp.exp(s - m)
            ls.append(jnp.sum(e, axis=-1, keepdims=True))
            ms.append(m)
            es.append(e.astype(BF16))
        o2 = jnp.dot(jnp.concatenate(es, axis=0), v, preferred_element_type=F32)
        stats = (jnp.where(head0, o2[:ATT_BLOCK], o2[ATT_BLOCK:]),
                 jnp.where(head0, ms[0], ms[1]),
                 jnp.where(head0, ls[0], ls[1]))
        for ref, val in zip((so_ref, sm_ref, sl_ref), stats):
            if ATT_GROUPS[g][1] > 1:
                ref[g, pl.ds(q_start, ATT_BLOCK), :] = val
                continue
            for u in range(ATT_BLOCK // SUBLANES):
                start = (u % 2) * SUBLANES * dense_pitch + idx * (ATT_BLOCK // MAX_DIL) + u // 2
                ref[g, pl.ds(start, SUBLANES, stride=dense_pitch), :] = (
                    val[u * SUBLANES:(u + 1) * SUBLANES])

    for g, (window, dil) in enumerate(ATT_GROUPS):
        blocks_per_class = n_blocks // dil

        def body(it, carry, g=g, blocks_per_class=blocks_per_class):
            for u in range(ATT_UNROLL):
                block(g, it * ATT_UNROLL + u, blocks_per_class)
            return carry

        lax.fori_loop(0, n_blocks // ATT_UNROLL, body, 0)

    rows = seq // MAX_DIL
    for c16 in range(MAX_DIL):
        idxs = []
        for g, (window, dil) in enumerate(ATT_GROUPS):
            if dil == 1:
                idxs.append(pl.ds(c16 * dense_pitch, rows))
            elif dil == MAX_DIL:
                idxs.append(pl.ds(c16 * rows, rows))
            else:
                idxs.append(pl.ds((c16 % dil) * (seq // dil) + c16 // dil, rows,
                                  stride=MAX_DIL // dil))
        ms_ = [sm_ref[g, idx, :] for g, idx in enumerate(idxs)]
        m = jnp.maximum(jnp.maximum(ms_[0], ms_[1]), ms_[2])
        ws = [jnp.exp(mg - m) for mg in ms_]
        num = sum(w * so_ref[g, idx, :] for g, (w, idx) in enumerate(zip(ws, idxs)))
        den = sum(w * sl_ref[g, idx, :] for g, (w, idx) in enumerate(zip(ws, idxs)))
        out_ref[pl.ds(c16, rows, stride=MAX_DIL), :] = num / den


def _attention(qkv_groups, bias, batch, seq):
    in_specs, args = [], []
    for g, (window, dil) in enumerate(ATT_GROUPS):
        arr = qkv_groups[g].reshape(3 * N_PAIRS, batch, seq, HEAD_PAIR)
        for part in range(3):
            in_specs.append(pl.BlockSpec(
                (None, None, seq, HEAD_PAIR),
                lambda b, p, part=part: (part * N_PAIRS + p, b, 0, 0)))
            args.append(arr)
    in_specs.append(pl.BlockSpec(
        (N_GROUPS, 2, 2, ATT_BLOCK, 2 * ATT_BLOCK), lambda b, p: (0, p, 0, 0, 0)))
    args.append(bias)
    scratch = pltpu.VMEM((N_GROUPS, seq + MAX_DIL * SUBLANES, HEAD_PAIR), F32)
    out = pl.pallas_call(
        _attn_kernel,
        out_shape=jax.ShapeDtypeStruct((N_PAIRS, batch, seq, HEAD_PAIR), F32),
        grid=(batch, N_PAIRS),
        in_specs=in_specs,
        out_specs=pl.BlockSpec((None, None, seq, HEAD_PAIR), lambda b, p: (p, b, 0, 0)),
        scratch_shapes=[scratch, scratch, scratch],
        compiler_params=pltpu.CompilerParams(
            dimension_semantics=("arbitrary", "arbitrary"), vmem_limit_bytes=VMEM_LIMIT),
        name="dilated_attention",
    )(*args)
    return out.reshape(N_PAIRS, batch * seq, HEAD_PAIR)


def _attention_bias():
    n = ATT_BLOCK
    heads = jnp.arange(1, N_QKV_HEADS + 1, dtype=F32)
    slopes = jnp.exp2(-ALIBI_MAX_BIAS * heads / N_QKV_HEADS).reshape(N_GROUPS, HEADS_PER_GROUP)
    qi = jnp.arange(n)[:, None]
    ki = jnp.arange(2 * n)[None, :]
    tables = []
    for g, (window, dil) in enumerate(ATT_GROUPS):
        steps = n + qi - ki
        valid = (steps >= 0) & (steps <= n)
        steps_f = qi - ki
        valid_f = steps_f >= 0
        per_head = []
        for hh in range(HEADS_PER_GROUP):
            sl = slopes[g, hh]
            reg = jnp.where(valid, -sl * (dil * steps).astype(F32), NEG_INF)
            fst = jnp.where(valid_f, -sl * (dil * steps_f).astype(F32), NEG_INF)
            per_head.append(jnp.stack([reg, fst]))
        tables.append(jnp.stack(per_head))
    return jnp.stack(tables).astype(F32)


def _block_diag(w):
    depth = w.shape[0]
    per = LRU_GATE_BLK // LRU_BLOCK
    w = w.reshape(depth, D_LRU // LRU_GATE_BLK, per, LRU_BLOCK, LRU_BLOCK)
    eye = jnp.eye(per, dtype=w.dtype)
    full = w[:, :, :, :, None, :] * eye[None, None, :, None, :, None]
    return full.reshape(depth, D_LRU // LRU_GATE_BLK, LRU_GATE_BLK, LRU_GATE_BLK)


def kernel(x, ffn1_norm, ffn1_w_gate, ffn1_w_up, ffn1_w_down, mix_norm, w_in, q_norm, k_norm, conv_w, conv_b, gate_a_w, gate_a_b, gate_x_w, gate_x_b, lru_lambda, w_out, ffn2_norm, ffn2_w_gate, ffn2_w_up, ffn2_w_down):
    batch, seq, _ = x.shape
    depth = w_in.shape[0]
    n = batch * seq

    w_in_b = w_in.astype(BF16)
    w_gates = (0.5 * jnp.concatenate([_block_diag(gate_a_w), _block_diag(gate_x_w)],
                                     axis=-1)).astype(BF16)
    gate_a_b = 0.5 * gate_a_b
    gate_x_b = 0.5 * gate_x_b
    head_id = jnp.arange(MXU_DIM) // HEAD_DIM
    head_mean = ((head_id[:, None] == head_id[None, :]).astype(F32) / HEAD_DIM).astype(BF16)
    score_scale = 1.0 / math.sqrt(HEAD_DIM)
    q_gain = (jnp.tile(q_norm, (1, N_QKV_HEADS)) * score_scale).reshape(depth, 1, D_ATT_QKV)
    k_gain = jnp.tile(k_norm, (1, N_QKV_HEADS)).reshape(depth, 1, D_ATT_QKV)
    bias = _attention_bias()

    def rows(v):
        return v.reshape(depth, 1, v.shape[-1])

    mixer_params = (rows(mix_norm), w_in_b, head_mean, q_gain, k_gain, conv_w, rows(conv_b),
                    w_gates, rows(gate_a_b), rows(gate_x_b), rows(lru_lambda))
    xf = x.reshape(n, D_MODEL)
    for l in range(depth):
        xf = _ffn(xf, l, ffn1_norm, ffn1_w_gate, ffn1_w_up, ffn1_w_down)
        qkv0, qkv1, qkv2, y = _mixer_in(xf, l, mixer_params, batch, seq)
        att = _attention((qkv0, qkv1, qkv2), bias, batch, seq)
        xf = _ffn(xf, l, ffn2_norm, ffn2_w_gate, ffn2_w_up, ffn2_w_down,
                  mixer_out=(att, y, w_out))
    return xf.reshape(batch, seq, D_MODEL)
```

```python
import functools
import math

import jax
import jax.numpy as jnp
from jax import lax
from jax.experimental import pallas as pl
from jax.experimental.pallas import tpu as pltpu

F32 = jnp.float32
BF16 = jnp.bfloat16

D_MODEL = 1024
HEAD_DIM = 64
ATT_GROUPS = ((128, 1), (512, 4), (2048, 16))
N_GROUPS = len(ATT_GROUPS)
HEADS_PER_GROUP = 4
N_QKV_HEADS = N_GROUPS * HEADS_PER_GROUP
D_ATT_QKV = N_QKV_HEADS * HEAD_DIM
D_ATT_OUT = HEADS_PER_GROUP * HEAD_DIM
ALIBI_MAX_BIAS = 8.0
D_LRU = 768
LRU_BLOCK = 64
CONV_WIDTH = 4
LRU_C = 8.0
D_FF = 2816
FFN_RES_WEIGHT = 0.5
RMS_EPS = 1e-6
NEG_INF = -1e30

LANES = 128
SUBLANES = 8
MXU_DIM = 256
VMEM_LIMIT = 56 * 1024 * 1024

FF_CHUNK = MXU_DIM
N_FF_CHUNKS = D_FF // FF_CHUNK
TM_FFN = 1024
TM_FFN_MIX = 512
TM_PROJ = 1024
ATT_BLOCK = 128
HEAD_PAIR = LANES
N_PAIRS = D_ATT_OUT // HEAD_PAIR
MAX_DIL = ATT_GROUPS[-1][1]
ATT_UNROLL = 32
HEADS_PER_MXU = MXU_DIM // HEAD_DIM
LRU_GATE_BLK = MXU_DIM
N_LRU_LANE_TILES = D_LRU // LANES
LRU_SEG = TM_PROJ // SUBLANES
LRU_PITCH = LRU_SEG + SUBLANES
CLS16_PITCH = TM_PROJ // MAX_DIL + SUBLANES


def _resident(block_shape, index_map):
    return pl.BlockSpec(block_shape, index_map, pipeline_mode=pl.Buffered(1))


def _rms_norm(x, gain):
    ms = jnp.mean(x * x, axis=-1, keepdims=True)
    return x * lax.rsqrt(ms + RMS_EPS) * gain


def _ffn_kernel(*refs, with_mixer_out):
    if with_mixer_out:
        x_ref, att_ref, y_ref, wo_ref, gain_ref, wg_ref, wu_ref, wd_ref, o_ref = refs
        mix = jnp.concatenate(
            [att_ref[p].astype(BF16) for p in range(N_PAIRS)] + [y_ref[...]], axis=-1)
        x = x_ref[...] + jnp.dot(mix, wo_ref[...].astype(BF16),
                                 preferred_element_type=F32)
    else:
        x_ref, gain_ref, wg_ref, wu_ref, wd_ref, o_ref = refs
        x = x_ref[...]
    h = _rms_norm(x, gain_ref[...]).astype(BF16)
    for j in range(N_FF_CHUNKS):
        cols = slice(j * FF_CHUNK, (j + 1) * FF_CHUNK)
        g = jnp.dot(h, wg_ref[:, cols].astype(BF16), preferred_element_type=F32)
        u = jnp.dot(h, wu_ref[:, cols].astype(BF16), preferred_element_type=F32)
        a = (g / (1.0 + jnp.exp(-g)) * u).astype(BF16)
        d = jnp.dot(a, wd_ref[cols, :].astype(BF16), preferred_element_type=F32)
        if j == 0:
            o_ref[...] = d
        else:
            o_ref[...] += d
    o_ref[...] = x + FFN_RES_WEIGHT * o_ref[...]


def _ffn(x, layer, gain, wg, wu, wd, mixer_out=None):
    n = x.shape[0]
    tm = TM_FFN if mixer_out is None else TM_FFN_MIX
    row_tile = pl.BlockSpec((tm, D_MODEL), lambda i: (i, 0))

    def layer_slice(rows, cols):
        return _resident((None, rows, cols), lambda i: (layer, 0, 0))

    in_specs, args = [row_tile], [x]
    if mixer_out is not None:
        att, y, w_out = mixer_out
        in_specs += [
            pl.BlockSpec((N_PAIRS, tm, HEAD_PAIR), lambda i: (0, i, 0)),
            pl.BlockSpec((tm, D_LRU), lambda i: (i, 0)),
            layer_slice(D_ATT_OUT + D_LRU, D_MODEL),
        ]
        args += [att, y, w_out]
    in_specs += [
        layer_slice(1, D_MODEL),
        layer_slice(D_MODEL, D_FF),
        layer_slice(D_MODEL, D_FF),
        layer_slice(D_FF, D_MODEL),
    ]
    args += [gain.reshape(gain.shape[0], 1, D_MODEL), wg, wu, wd]
    return pl.pallas_call(
        functools.partial(_ffn_kernel, with_mixer_out=mixer_out is not None),
        out_shape=jax.ShapeDtypeStruct((n, D_MODEL), F32),
        grid=(n // tm,),
        in_specs=in_specs,
        out_specs=row_tile,
        compiler_params=pltpu.CompilerParams(
            dimension_semantics=("arbitrary",), vmem_limit_bytes=VMEM_LIMIT),
        name="ffn_mixer_out" if mixer_out is not None else "ffn",
    )(*args)


def _mixer_in_kernel(x_ref, gain_ref, w_ref, hm_ref, qg_ref, kg_ref,
                     cw_ref, cb_ref, wgate_ref, ba_ref, bx_ref, lam_ref,
                     o0_ref, o1_ref, o2_ref, y_ref,
                     slab_ref, slab16_ref, seg_ref, hist_ref, h_ref):
    @pl.when(pl.program_id(1) == 0)
    def _():
        hist_ref[...] = jnp.zeros(hist_ref.shape, F32)
        h_ref[...] = jnp.zeros(h_ref.shape, F32)

    x = x_ref[...]
    h = _rms_norm(x, gain_ref[...]).astype(BF16)

    def head_norm(t, gain):
        sq = (t * t).astype(BF16)
        ms = jnp.concatenate(
            [jnp.dot(sq[:, i * MXU_DIM:(i + 1) * MXU_DIM], hm_ref[...],
                     preferred_element_type=F32)
             for i in range(D_ATT_QKV // MXU_DIM)], axis=-1)
        return t * lax.rsqrt(ms + RMS_EPS) * gain

    c0 = 3 * D_ATT_QKV
    lru_x = jnp.dot(h, w_ref[:, c0:c0 + D_LRU], preferred_element_type=F32)
    q = jnp.dot(h, w_ref[:, 0:D_ATT_QKV], preferred_element_type=F32)

    sub = lax.broadcasted_iota(jnp.int32, (SUBLANES, LANES), 0)
    seg_rows = [slice(i * SUBLANES, (i + 1) * SUBLANES) for i in range(LRU_SEG)]

    def to_segment_order(col, lt):
        for s in range(SUBLANES):
            seg_ref[lt, pl.ds(s * LRU_PITCH, LRU_SEG), :] = col[s * LRU_SEG:(s + 1) * LRU_SEG]
        return [seg_ref[lt, pl.ds(i, SUBLANES, stride=LRU_PITCH), :] for i in range(LRU_SEG)]

    def to_time_order(vregs, lt):
        for i, vreg in enumerate(vregs):
            seg_ref[lt, pl.ds(i, SUBLANES, stride=LRU_PITCH), :] = vreg
        return jnp.concatenate(
            [seg_ref[lt, pl.ds(s * LRU_PITCH, LRU_SEG), :] for s in range(SUBLANES)], axis=0)

    n_hist = CONV_WIDTH - 1
    xc_cols = []
    for lt in range(N_LRU_LANE_TILES):
        lanes = slice(lt * LANES, (lt + 1) * LANES)
        xv = to_segment_order(lru_x[:, lanes], lt)
        head = []
        for j in range(n_hist):
            last = xv[LRU_SEG - n_hist + j]
            prev = hist_ref[lt, seg_rows[j], :]
            head.append(jnp.where(sub == 0, pltpu.roll(prev, 1, 0), pltpu.roll(last, 1, 0)))
            hist_ref[lt, seg_rows[j], :] = last
        ext = jnp.concatenate(head + xv, axis=0)
        xc_l = cb_ref[:, lanes]
        for w in range(CONV_WIDTH):
            xc_l = xc_l + cw_ref[w:w + 1, lanes] * ext[w * SUBLANES:w * SUBLANES + TM_PROJ]
        xc_cols.append(xc_l)
    xc = jnp.concatenate(xc_cols, axis=-1)

    xcb = xc.astype(BF16)
    gates = [jnp.dot(xcb[:, i * LRU_GATE_BLK:(i + 1) * LRU_GATE_BLK], wgate_ref[i],
                     preferred_element_type=F32)
             for i in range(D_LRU // LRU_GATE_BLK)]
    k = jnp.dot(h, w_ref[:, D_ATT_QKV:2 * D_ATT_QKV], preferred_element_type=F32)
    gate = jnp.dot(h, w_ref[:, c0 + D_LRU:c0 + 2 * D_LRU], preferred_element_type=F32)
    v = jnp.dot(h, w_ref[:, 2 * D_ATT_QKV:3 * D_ATT_QKV], preferred_element_type=F32)
    tanh_a = jnp.tanh(jnp.concatenate([t[:, :LRU_GATE_BLK] for t in gates], axis=-1)
                      + ba_ref[...])
    tanh_x = jnp.tanh(jnp.concatenate([t[:, LRU_GATE_BLK:] for t in gates], axis=-1)
                      + bx_ref[...])
    z = -lam_ref[...]
    softplus = jnp.maximum(z, 0.0) + jnp.log1p(jnp.exp(-jnp.abs(z)))
    half_decay = (-0.5 * LRU_C * math.log2(math.e)) * softplus
    a = jnp.exp2(half_decay + half_decay * tanh_a)
    half_xc = 0.5 * xc
    s = 1.0 - a * a
    root = jnp.where(s > 0.0, s * lax.rsqrt(s), 0.0)
    b = root * (half_xc + half_xc * tanh_x)

    h_cols = []
    for lt in range(N_LRU_LANE_TILES):
        lanes = slice(lt * LANES, (lt + 1) * LANES)
        a_l, b_l = a[:, lanes], b[:, lanes]
        h_loc = jnp.zeros((SUBLANES, LANES), F32)
        a_cum = jnp.ones((SUBLANES, LANES), F32)
        h_locs, a_cums = [], []
        for rows in seg_rows:
            h_loc = a_l[rows] * h_loc + b_l[rows]
            a_cum = a_cum * a_l[rows]
            h_locs.append(h_loc)
            a_cums.append(a_cum)
        carry = h_ref[lt, 0:1, :]
        carry_in = jnp.zeros((SUBLANES, LANES), F32)
        for s in range(SUBLANES):
            carry_in = jnp.where(sub == s, carry, carry_in)
            carry = a_cum[s:s + 1, :] * carry + h_loc[s:s + 1, :]
        h_ref[lt, 0:1, :] = carry
        h_cols.append(to_time_order(
            [hl + ac * carry_in for hl, ac in zip(h_locs, a_cums)], lt))
    h_all = jnp.concatenate(h_cols, axis=-1)

    inner = math.sqrt(2.0 / math.pi) * (gate + 0.044715 * (gate * gate * gate))
    gelu = 0.5 * gate * (1.0 + jnp.tanh(inner))
    y_ref[...] = (h_all * gelu).astype(y_ref.dtype)

    qn = head_norm(q, qg_ref[...])
    kn = head_norm(k, kg_ref[...])
    outs = (o0_ref, o1_ref, o2_ref)
    for g, (window, dil) in enumerate(ATT_GROUPS):
        for part, val in enumerate((qn, kn, v)):
            for p in range(N_PAIRS):
                c = g * D_ATT_OUT + p * HEAD_PAIR
                piece = val[:, c:c + HEAD_PAIR]
                dst = outs[g].at[part * N_PAIRS + p]
                slab = part * N_PAIRS + p
                if dil == 1:
                    dst[0] = piece.astype(BF16)
                elif dil == SUBLANES * 2:
                    per_class = TM_PROJ // dil
                    for i in range(TM_PROJ // SUBLANES):
                        start = (i % 2) * SUBLANES * CLS16_PITCH + i // 2
                        slab16_ref[slab, pl.ds(start, SUBLANES, stride=CLS16_PITCH), :] = (
                            piece[i * SUBLANES:(i + 1) * SUBLANES])
                    for cls in range(dil):
                        dst[cls] = slab16_ref[
                            slab, pl.ds(cls * CLS16_PITCH, per_class), :].astype(BF16)
                else:
                    slab_ref[slab] = piece
                    for cls in range(dil):
                        rows = pl.ds(cls, TM_PROJ // dil, stride=dil)
                        dst[cls] = slab_ref[slab, rows, :].astype(BF16)


def _mixer_in(x, layer, params, batch, seq):
    n = x.shape[0]
    tiles_per_seq = seq // TM_PROJ

    def token_tile(width):
        return pl.BlockSpec((TM_PROJ, width), lambda b, t: (b * tiles_per_seq + t, 0))

    def param_spec(p):
        if p.ndim == 2:
            return _resident(p.shape, lambda b, t: (0, 0))
        zeros = (0,) * (p.ndim - 1)
        return _resident((None, *p.shape[1:]), lambda b, t: (layer, *zeros))

    qkv_shapes, qkv_specs = [], []
    for window, dil in ATT_GROUPS:
        qkv_shapes.append(jax.ShapeDtypeStruct(
            (3 * N_PAIRS, batch, dil, seq // dil, HEAD_PAIR), BF16))
        qkv_specs.append(pl.BlockSpec(
            (3 * N_PAIRS, None, dil, TM_PROJ // dil, HEAD_PAIR),
            lambda b, t: (0, b, 0, t, 0)))
    return pl.pallas_call(
        _mixer_in_kernel,
        out_shape=(*qkv_shapes, jax.ShapeDtypeStruct((n, D_LRU), BF16)),
        grid=(batch, tiles_per_seq),
        in_specs=[token_tile(D_MODEL)] + [param_spec(p) for p in params],
        out_specs=(*qkv_specs, token_tile(D_LRU)),
        scratch_shapes=[
            pltpu.VMEM((3 * N_PAIRS, TM_PROJ, HEAD_PAIR), F32),
            pltpu.VMEM((3 * N_PAIRS, MAX_DIL * CLS16_PITCH, HEAD_PAIR), F32),
            pltpu.VMEM((N_LRU_LANE_TILES, SUBLANES * LRU_PITCH, LANES), F32),
            pltpu.VMEM((N_LRU_LANE_TILES, (CONV_WIDTH - 1) * SUBLANES, LANES), F32),
            pltpu.VMEM((N_LRU_LANE_TILES, SUBLANES, LANES), F32),
        ],
        compiler_params=pltpu.CompilerParams(
            dimension_semantics=("arbitrary", "arbitrary"), vmem_limit_bytes=VMEM_LIMIT),
        name="mixer_in",
    )(x, *params)


def _attn_kernel(q0_ref, k0_ref, v0_ref, q1_ref, k1_ref, v1_ref,
                 q2_ref, k2_ref, v2_ref, bias_ref, out_ref, so_ref, sm_ref, sl_ref):
    qkv = ((q0_ref, k0_ref, v0_ref), (q1_ref, k1_ref, v1_ref), (q2_ref, k2_ref, v2_ref))
    seq = out_ref.shape[0]
    n_blocks = seq // ATT_BLOCK
    dense_pitch = seq // MAX_DIL + SUBLANES
    lane = lax.broadcasted_iota(jnp.int32, (ATT_BLOCK, HEAD_PAIR), 1)
    head0 = lane < HEAD_DIM

    def block(g, idx, blocks_per_class):
        q_ref, k_ref, v_ref = qkv[g]
        first = (idx & (blocks_per_class - 1)) == 0
        q_start = pl.multiple_of(idx * ATT_BLOCK, ATT_BLOCK)
        k_start = pl.multiple_of(jnp.where(first, idx, idx - 1) * ATT_BLOCK, ATT_BLOCK)
        table = jnp.where(first, 1, 0)
        q = q_ref[pl.ds(q_start, ATT_BLOCK), :]
        k = k_ref[pl.ds(k_start, 2 * ATT_BLOCK), :]
        v = v_ref[pl.ds(k_start, 2 * ATT_BLOCK), :]
        zero = jnp.zeros_like(q)
        q2 = jnp.concatenate([jnp.where(head0, q, zero), jnp.where(head0, zero, q)], axis=0)
        s2 = lax.dot_general(q2, k, (((1,), (1,)), ((), ())), preferred_element_type=F32)
        es, ms, ls = [], [], []
        for hh in range(2):
            s = s2[hh * ATT_BLOCK:(hh + 1) * ATT_BLOCK] + bias_ref[g, hh, table]
            m = jnp.max(s, axis=-1, keepdims=True)
            e = jnp.exp(s - m)
            ls.append(jnp.sum(e, axis=-1, keepdims=True))
            ms.append(m)
            es.append(e.astype(BF16))
        o2 = jnp.dot(jnp.concatenate(es, axis=0), v, preferred_element_type=F32)
        stats = (jnp.where(head0, o2[:ATT_BLOCK], o2[ATT_BLOCK:]),
                 jnp.where(head0, ms[0], ms[1]),
                 jnp.where(head0, ls[0], ls[1]))
        for ref, val in zip((so_ref, sm_ref, sl_ref), stats):
            if ATT_GROUPS[g][1] > 1:
                ref[g, pl.ds(q_start, ATT_BLOCK), :] = val
                continue
            for u in range(ATT_BLOCK // SUBLANES):
                start = (u % 2) * SUBLANES * dense_pitch + idx * (ATT_BLOCK // MAX_DIL) + u // 2
                ref[g, pl.ds(start, SUBLANES, stride=dense_pitch), :] = (
                    val[u * SUBLANES:(u + 1) * SUBLANES])

    for g, (window, dil) in enumerate(ATT_GROUPS):
        blocks_per_class = n_blocks // dil

        def body(it, carry, g=g, blocks_per_class=blocks_per_class):
            for u in range(ATT_UNROLL):
                block(g, it * ATT_UNROLL + u, blocks_per_class)
            return carry

        lax.fori_loop(0, n_blocks // ATT_UNROLL, body, 0)

    rows = seq // MAX_DIL
    for c16 in range(MAX_DIL):
        idxs = []
        for g, (window, dil) in enumerate(ATT_GROUPS):
            if dil == 1:
                idxs.append(pl.ds(c16 * dense_pitch, rows))
            elif dil == MAX_DIL:
                idxs.append(pl.ds(c16 * rows, rows))
            else:
                idxs.append(pl.ds((c16 % dil) * (seq // dil) + c16 // dil, rows,
                                  stride=MAX_DIL // dil))
        ms_ = [sm_ref[g, idx, :] for g, idx in enumerate(idxs)]
        m = jnp.maximum(jnp.maximum(ms_[0], ms_[1]), ms_[2])
        ws = [jnp.exp(mg - m) for mg in ms_]
        num = sum(w * so_ref[g, idx, :] for g, (w, idx) in enumerate(zip(ws, idxs)))
        den = sum(w * sl_ref[g, idx, :] for g, (w, idx) in enumerate(zip(ws, idxs)))
        out_ref[pl.ds(c16, rows, stride=MAX_DIL), :] = num / den


def _attention(qkv_groups, bias, batch, seq):
    in_specs, args = [], []
    for g, (window, dil) in enumerate(ATT_GROUPS):
        arr = qkv_groups[g].reshape(3 * N_PAIRS, batch, seq, HEAD_PAIR)
        for part in range(3):
            in_specs.append(pl.BlockSpec(
                (None, None, seq, HEAD_PAIR),
                lambda b, p, part=part: (part * N_PAIRS + p, b, 0, 0)))
            args.append(arr)
    in_specs.append(pl.BlockSpec(
        (N_GROUPS, 2, 2, ATT_BLOCK, 2 * ATT_BLOCK), lambda b, p: (0, p, 0, 0, 0)))
    args.append(bias)
    scratch = pltpu.VMEM((N_GROUPS, seq + MAX_DIL * SUBLANES, HEAD_PAIR), F32)
    out = pl.pallas_call(
        _attn_kernel,
        out_shape=jax.ShapeDtypeStruct((N_PAIRS, batch, seq, HEAD_PAIR), F32),
        grid=(batch, N_PAIRS),
        in_specs=in_specs,
        out_specs=pl.BlockSpec((None, None, seq, HEAD_PAIR), lambda b, p: (p, b, 0, 0)),
        scratch_shapes=[scratch, scratch, scratch],
        compiler_params=pltpu.CompilerParams(
            dimension_semantics=("arbitrary", "arbitrary"), vmem_limit_bytes=VMEM_LIMIT),
        name="dilated_attention",
    )(*args)
    return out.reshape(N_PAIRS, batch * seq, HEAD_PAIR)


def _attention_bias():
    n = ATT_BLOCK
    heads = jnp.arange(1, N_QKV_HEADS + 1, dtype=F32)
    slopes = jnp.exp2(-ALIBI_MAX_BIAS * heads / N_QKV_HEADS).reshape(N_GROUPS, HEADS_PER_GROUP)
    qi = jnp.arange(n)[:, None]
    ki = jnp.arange(2 * n)[None, :]
    tables = []
    for g, (window, dil) in enumerate(ATT_GROUPS):
        steps = n + qi - ki
        valid = (steps >= 0) & (steps <= n)
        steps_f = qi - ki
        valid_f = steps_f >= 0
        per_head = []
        for hh in range(HEADS_PER_GROUP):
            sl = slopes[g, hh]
            reg = jnp.where(valid, -sl * (dil * steps).astype(F32), NEG_INF)
            fst = jnp.where(valid_f, -sl * (dil * steps_f).astype(F32), NEG_INF)
            per_head.append(jnp.stack([reg, fst]))
        tables.append(jnp.stack(per_head))
    return jnp.stack(tables).astype(F32)


def _block_diag(w):
    depth = w.shape[0]
    per = LRU_GATE_BLK // LRU_BLOCK
    w = w.reshape(depth, D_LRU // LRU_GATE_BLK, per, LRU_BLOCK, LRU_BLOCK)
    eye = jnp.eye(per, dtype=w.dtype)
    full = w[:, :, :, :, None, :] * eye[None, None, :, None, :, None]
    return full.reshape(depth, D_LRU // LRU_GATE_BLK, LRU_GATE_BLK, LRU_GATE_BLK)


def kernel(x, ffn1_norm, ffn1_w_gate, ffn1_w_up, ffn1_w_down, mix_norm, w_in, q_norm, k_norm, conv_w, conv_b, gate_a_w, gate_a_b, gate_x_w, gate_x_b, lru_lambda, w_out, ffn2_norm, ffn2_w_gate, ffn2_w_up, ffn2_w_down):
    batch, seq, _ = x.shape
    depth = w_in.shape[0]
    n = batch * seq

    w_in_b = w_in.astype(BF16)
    w_gates = (0.5 * jnp.concatenate([_block_diag(gate_a_w), _block_diag(gate_x_w)],
                                     axis=-1)).astype(BF16)
    gate_a_b = 0.5 * gate_a_b
    gate_x_b = 0.5 * gate_x_b
    head_id = jnp.arange(MXU_DIM) // HEAD_DIM
    head_mean = ((head_id[:, None] == head_id[None, :]).astype(F32) / HEAD_DIM).astype(BF16)
    score_scale = 1.0 / math.sqrt(HEAD_DIM)
    q_gain = (jnp.tile(q_norm, (1, N_QKV_HEADS)) * score_scale).reshape(depth, 1, D_ATT_QKV)
    k_gain = jnp.tile(k_norm, (1, N_QKV_HEADS)).reshape(depth, 1, D_ATT_QKV)
    bias = _attention_bias()

    def rows(v):
        return v.reshape(depth, 1, v.shape[-1])

    mixer_params = (rows(mix_norm), w_in_b, head_mean, q_gain, k_gain, conv_w, rows(conv_b),
                    w_gates, rows(gate_a_b), rows(gate_x_b), rows(lru_lambda))
    xf = x.reshape(n, D_MODEL)
    for l in range(depth):
        xf = _ffn(xf, l, ffn1_norm, ffn1_w_gate, ffn1_w_up, ffn1_w_down)
        qkv0, qkv1, qkv2, y = _mixer_in(xf, l, mixer_params, batch, seq)
        att = _attention((qkv0, qkv1, qkv2), bias, batch, seq)
        xf = _ffn(xf, l, ffn2_norm, ffn2_w_gate, ffn2_w_up, ffn2_w_down,
                  mixer_out=(att, y, w_out))
    return xf.reshape(batch, seq, D_MODEL)
```

```python
import functools
import math

import jax
import jax.numpy as jnp
from jax import lax
from jax.experimental import pallas as pl
from jax.experimental.pallas import tpu as pltpu

F32 = jnp.float32
BF16 = jnp.bfloat16

D_MODEL = 1024
HEAD_DIM = 64
ATT_GROUPS = ((128, 1), (512, 4), (2048, 16))
N_GROUPS = len(ATT_GROUPS)
HEADS_PER_GROUP = 4
N_QKV_HEADS = N_GROUPS * HEADS_PER_GROUP
D_ATT_QKV = N_QKV_HEADS * HEAD_DIM
D_ATT_OUT = HEADS_PER_GROUP * HEAD_DIM
ALIBI_MAX_BIAS = 8.0
D_LRU = 768
LRU_BLOCK = 64
CONV_WIDTH = 4
LRU_C = 8.0
D_FF = 2816
FFN_RES_WEIGHT = 0.5
RMS_EPS = 1e-6
NEG_INF = -1e30

LANES = 128
SUBLANES = 8
MXU_DIM = 256
VMEM_LIMIT = 56 * 1024 * 1024

FF_CHUNK = MXU_DIM
N_FF_CHUNKS = D_FF // FF_CHUNK
TM_FFN = 1024
TM_FFN_MIX = 512
TM_PROJ = 1024
ATT_BLOCK = 128
HEAD_PAIR = LANES
N_PAIRS = D_ATT_OUT // HEAD_PAIR
MAX_DIL = ATT_GROUPS[-1][1]
ATT_UNROLL = 32
LRU_GATE_BLK = MXU_DIM
N_LRU_LANE_TILES = D_LRU // LANES
LRU_SEG = TM_PROJ // SUBLANES
LRU_PITCH = LRU_SEG + SUBLANES
assert LRU_PITCH % (2 * SUBLANES) == SUBLANES
CLS16_PITCH = TM_PROJ // MAX_DIL + SUBLANES
assert CLS16_PITCH % (2 * SUBLANES) == SUBLANES


def _resident(block_shape, index_map):
    return pl.BlockSpec(block_shape, index_map, pipeline_mode=pl.Buffered(1))


def _rms_norm(x, gain):
    ms = jnp.mean(x * x, axis=-1, keepdims=True)
    return x * lax.rsqrt(ms + RMS_EPS) * gain


def _ffn_kernel(*refs, with_mixer_out):
    if with_mixer_out:
        x_ref, att_ref, y_ref, wo_ref, gain_ref, wg_ref, wu_ref, wd_ref, o_ref = refs
        mix = jnp.concatenate(
            [att_ref[p].astype(BF16) for p in range(N_PAIRS)] + [y_ref[...]], axis=-1)
        x = x_ref[...] + jnp.dot(mix, wo_ref[...].astype(BF16),
                                 preferred_element_type=F32)
    else:
        x_ref, gain_ref, wg_ref, wu_ref, wd_ref, o_ref = refs
        x = x_ref[...]
    h = _rms_norm(x, gain_ref[...]).astype(BF16)
    for j in range(N_FF_CHUNKS):
        cols = slice(j * FF_CHUNK, (j + 1) * FF_CHUNK)
        g = jnp.dot(h, wg_ref[:, cols].astype(BF16), preferred_element_type=F32)
        u = jnp.dot(h, wu_ref[:, cols].astype(BF16), preferred_element_type=F32)
        a = (g / (1.0 + jnp.exp(-g)) * u).astype(BF16)
        d = jnp.dot(a, wd_ref[cols, :].astype(BF16), preferred_element_type=F32)
        if j == 0:
            o_ref[...] = d
        else:
            o_ref[...] += d
    o_ref[...] = x + FFN_RES_WEIGHT * o_ref[...]


def _ffn(x, layer, gain, wg, wu, wd, mixer_out=None):
    n = x.shape[0]
    tm = TM_FFN if mixer_out is None else TM_FFN_MIX
    row_tile = pl.BlockSpec((tm, D_MODEL), lambda i: (i, 0))

    def layer_slice(rows, cols):
        return _resident((None, rows, cols), lambda i: (layer, 0, 0))

    in_specs, args = [row_tile], [x]
    if mixer_out is not None:
        att, y, w_out = mixer_out
        in_specs += [
            pl.BlockSpec((N_PAIRS, tm, HEAD_PAIR), lambda i: (0, i, 0)),
            pl.BlockSpec((tm, D_LRU), lambda i: (i, 0)),
            layer_slice(D_ATT_OUT + D_LRU, D_MODEL),
        ]
        args += [att, y, w_out]
    in_specs += [
        layer_slice(1, D_MODEL),
        layer_slice(D_MODEL, D_FF),
        layer_slice(D_MODEL, D_FF),
        layer_slice(D_FF, D_MODEL),
    ]
    args += [gain.reshape(gain.shape[0], 1, D_MODEL), wg, wu, wd]
    return pl.pallas_call(
        functools.partial(_ffn_kernel, with_mixer_out=mixer_out is not None),
        out_shape=jax.ShapeDtypeStruct((n, D_MODEL), F32),
        grid=(n // tm,),
        in_specs=in_specs,
        out_specs=row_tile,
        compiler_params=pltpu.CompilerParams(
            dimension_semantics=("arbitrary",), vmem_limit_bytes=VMEM_LIMIT),
        name="ffn_mixer_out" if mixer_out is not None else "ffn",
    )(*args)


def _mixer_in_kernel(x_ref, gain_ref, w_ref, hm_ref, qg_ref, kg_ref,
                     cw_ref, cb_ref, wgate_ref, ba_ref, bx_ref, lam_ref,
                     o0_ref, o1_ref, o2_ref, y_ref,
                     slab_ref, slab16_ref, seg_ref, hist_ref, h_ref):
    @pl.when(pl.program_id(1) == 0)
    def _():
        hist_ref[...] = jnp.zeros(hist_ref.shape, F32)
        h_ref[...] = jnp.zeros(h_ref.shape, F32)

    x = x_ref[...]
    h = _rms_norm(x, gain_ref[...]).astype(BF16)

    def head_norm(t, gain):
        sq = (t * t).astype(BF16)
        ms = jnp.concatenate(
            [jnp.dot(sq[:, i * MXU_DIM:(i + 1) * MXU_DIM], hm_ref[...],
                     preferred_element_type=F32)
             for i in range(D_ATT_QKV // MXU_DIM)], axis=-1)
        return t * lax.rsqrt(ms + RMS_EPS) * gain

    c0 = 3 * D_ATT_QKV
    lru_x = jnp.dot(h, w_ref[:, c0:c0 + D_LRU], preferred_element_type=F32)
    q = jnp.dot(h, w_ref[:, 0:D_ATT_QKV], preferred_element_type=F32)

    sub = lax.broadcasted_iota(jnp.int32, (SUBLANES, LANES), 0)
    seg_rows = [slice(i * SUBLANES, (i + 1) * SUBLANES) for i in range(LRU_SEG)]

    def to_segment_order(col, lt):
        for s in range(SUBLANES):
            seg_ref[lt, pl.ds(s * LRU_PITCH, LRU_SEG), :] = col[s * LRU_SEG:(s + 1) * LRU_SEG]
        return [seg_ref[lt, pl.ds(i, SUBLANES, stride=LRU_PITCH), :] for i in range(LRU_SEG)]

    def to_time_order(vregs, lt):
        for i, vreg in enumerate(vregs):
            seg_ref[lt, pl.ds(i, SUBLANES, stride=LRU_PITCH), :] = vreg
        return jnp.concatenate(
            [seg_ref[lt, pl.ds(s * LRU_PITCH, LRU_SEG), :] for s in range(SUBLANES)], axis=0)

    n_hist = CONV_WIDTH - 1
    xc_cols = []
    for lt in range(N_LRU_LANE_TILES):
        lanes = slice(lt * LANES, (lt + 1) * LANES)
        xv = to_segment_order(lru_x[:, lanes], lt)
        head = []
        for j in range(n_hist):
            last = xv[LRU_SEG - n_hist + j]
            prev = hist_ref[lt, seg_rows[j], :]
            head.append(jnp.where(sub == 0, pltpu.roll(prev, 1, 0), pltpu.roll(last, 1, 0)))
            hist_ref[lt, seg_rows[j], :] = last
        ext = jnp.concatenate(head + xv, axis=0)
        xc_l = cb_ref[:, lanes]
        for w in range(CONV_WIDTH):
            xc_l = xc_l + cw_ref[w:w + 1, lanes] * ext[w * SUBLANES:w * SUBLANES + TM_PROJ]
        xc_cols.append(xc_l)
    xc = jnp.concatenate(xc_cols, axis=-1)

    xcb = xc.astype(BF16)
    gates = [jnp.dot(xcb[:, i * LRU_GATE_BLK:(i + 1) * LRU_GATE_BLK], wgate_ref[i],
                     preferred_element_type=F32)
             for i in range(D_LRU // LRU_GATE_BLK)]
    k = jnp.dot(h, w_ref[:, D_ATT_QKV:2 * D_ATT_QKV], preferred_element_type=F32)
    gate = jnp.dot(h, w_ref[:, c0 + D_LRU:c0 + 2 * D_LRU], preferred_element_type=F32)
    v = jnp.dot(h, w_ref[:, 2 * D_ATT_QKV:3 * D_ATT_QKV], preferred_element_type=F32)
    tanh_a = jnp.tanh(jnp.concatenate([t[:, :LRU_GATE_BLK] for t in gates], axis=-1)
                      + ba_ref[...])
    tanh_x = jnp.tanh(jnp.concatenate([t[:, LRU_GATE_BLK:] for t in gates], axis=-1)
                      + bx_ref[...])
    z = -lam_ref[...]
    softplus = jnp.maximum(z, 0.0) + jnp.log1p(jnp.exp(-jnp.abs(z)))
    half_decay = (-0.5 * LRU_C * math.log2(math.e)) * softplus
    a = jnp.exp2(half_decay + half_decay * tanh_a)
    half_xc = 0.5 * xc
    s = 1.0 - a * a
    root = jnp.where(s > 0.0, s * lax.rsqrt(s), 0.0)
    b = root * (half_xc + half_xc * tanh_x)

    h_cols = []
    for lt in range(N_LRU_LANE_TILES):
        lanes = slice(lt * LANES, (lt + 1) * LANES)
        a_l, b_l = a[:, lanes], b[:, lanes]
        h_loc = jnp.zeros((SUBLANES, LANES), F32)
        a_cum = jnp.ones((SUBLANES, LANES), F32)
        h_locs, a_cums = [], []
        for rows in seg_rows:
            h_loc = a_l[rows] * h_loc + b_l[rows]
            a_cum = a_cum * a_l[rows]
            h_locs.append(h_loc)
            a_cums.append(a_cum)
        carry = h_ref[lt, 0:1, :]
        carry_in = jnp.zeros((SUBLANES, LANES), F32)
        for s in range(SUBLANES):
            carry_in = jnp.where(sub == s, carry, carry_in)
            carry = a_cum[s:s + 1, :] * carry + h_loc[s:s + 1, :]
        h_ref[lt, 0:1, :] = carry
        h_cols.append(to_time_order(
            [hl + ac * carry_in for hl, ac in zip(h_locs, a_cums)], lt))
    h_all = jnp.concatenate(h_cols, axis=-1)

    inner = math.sqrt(2.0 / math.pi) * (gate + 0.044715 * (gate * gate * gate))
    gelu = 0.5 * gate * (1.0 + jnp.tanh(inner))
    y_ref[...] = (h_all * gelu).astype(y_ref.dtype)

    qn = head_norm(q, qg_ref[...])
    kn = head_norm(k, kg_ref[...])
    outs = (o0_ref, o1_ref, o2_ref)
    for g, (window, dil) in enumerate(ATT_GROUPS):
        for part, val in enumerate((qn, kn, v)):
            for p in range(N_PAIRS):
                c = g * D_ATT_OUT + p * HEAD_PAIR
                piece = val[:, c:c + HEAD_PAIR]
                dst = outs[g].at[part * N_PAIRS + p]
                slab = part * N_PAIRS + p
                if dil == 1:
                    dst[0] = piece.astype(BF16)
                elif dil == SUBLANES * 2:
                    per_class = TM_PROJ // dil
                    for i in range(TM_PROJ // SUBLANES):
                        start = (i % 2) * SUBLANES * CLS16_PITCH + i // 2
                        slab16_ref[slab, pl.ds(start, SUBLANES, stride=CLS16_PITCH), :] = (
                            piece[i * SUBLANES:(i + 1) * SUBLANES])
                    for cls in range(dil):
                        dst[cls] = slab16_ref[
                            slab, pl.ds(cls * CLS16_PITCH, per_class), :].astype(BF16)
                else:
                    slab_ref[slab] = piece
                    for cls in range(dil):
                        rows = pl.ds(cls, TM_PROJ // dil, stride=dil)
                        dst[cls] = slab_ref[slab, rows, :].astype(BF16)


def _mixer_in(x, layer, params, batch, seq):
    n = x.shape[0]
    tiles_per_seq = seq // TM_PROJ

    def token_tile(width):
        return pl.BlockSpec((TM_PROJ, width), lambda b, t: (b * tiles_per_seq + t, 0))

    def param_spec(p):
        if p.ndim == 2:
            return _resident(p.shape, lambda b, t: (0, 0))
        zeros = (0,) * (p.ndim - 1)
        return _resident((None, *p.shape[1:]), lambda b, t: (layer, *zeros))

    qkv_shapes, qkv_specs = [], []
    for window, dil in ATT_GROUPS:
        qkv_shapes.append(jax.ShapeDtypeStruct(
            (3 * N_PAIRS, batch, dil, seq // dil, HEAD_PAIR), BF16))
        qkv_specs.append(pl.BlockSpec(
            (3 * N_PAIRS, None, dil, TM_PROJ // dil, HEAD_PAIR),
            lambda b, t: (0, b, 0, t, 0)))
    return pl.pallas_call(
        _mixer_in_kernel,
        out_shape=(*qkv_shapes, jax.ShapeDtypeStruct((n, D_LRU), BF16)),
        grid=(batch, tiles_per_seq),
        in_specs=[token_tile(D_MODEL)] + [param_spec(p) for p in params],
        out_specs=(*qkv_specs, token_tile(D_LRU)),
        scratch_shapes=[
            pltpu.VMEM((3 * N_PAIRS, TM_PROJ, HEAD_PAIR), F32),
            pltpu.VMEM((3 * N_PAIRS, MAX_DIL * CLS16_PITCH, HEAD_PAIR), F32),
            pltpu.VMEM((N_LRU_LANE_TILES, SUBLANES * LRU_PITCH, LANES), F32),
            pltpu.VMEM((N_LRU_LANE_TILES, (CONV_WIDTH - 1) * SUBLANES, LANES), F32),
            pltpu.VMEM((N_LRU_LANE_TILES, SUBLANES, LANES), F32),
        ],
        compiler_params=pltpu.CompilerParams(
            dimension_semantics=("arbitrary", "arbitrary"), vmem_limit_bytes=VMEM_LIMIT),
        name="mixer_in",
    )(x, *params)


def _attn_kernel(q0_ref, k0_ref, v0_ref, q1_ref, k1_ref, v1_ref,
                 q2_ref, k2_ref, v2_ref, bias_ref, out_ref, so_ref, sm_ref, sl_ref):
    qkv = ((q0_ref, k0_ref, v0_ref), (q1_ref, k1_ref, v1_ref), (q2_ref, k2_ref, v2_ref))
    seq = out_ref.shape[0]
    n_blocks = seq // ATT_BLOCK
    dense_pitch = seq // MAX_DIL + SUBLANES
    lane = lax.broadcasted_iota(jnp.int32, (ATT_BLOCK, HEAD_PAIR), 1)
    head0 = lane < HEAD_DIM

    def block(g, idx, blocks_per_class):
        q_ref, k_ref, v_ref = qkv[g]
        first = (idx & (blocks_per_class - 1)) == 0
        q_start = pl.multiple_of(idx * ATT_BLOCK, ATT_BLOCK)
        k_start = pl.multiple_of(jnp.where(first, idx, idx - 1) * ATT_BLOCK, ATT_BLOCK)
        table = jnp.where(first, 1, 0)
        q = q_ref[pl.ds(q_start, ATT_BLOCK), :]
        k = k_ref[pl.ds(k_start, 2 * ATT_BLOCK), :]
        v = v_ref[pl.ds(k_start, 2 * ATT_BLOCK), :]
        zero = jnp.zeros_like(q)
        q2 = jnp.concatenate([jnp.where(head0, q, zero), jnp.where(head0, zero, q)], axis=0)
        s2 = lax.dot_general(q2, k, (((1,), (1,)), ((), ())), preferred_element_type=F32)
        es, ms, ls = [], [], []
        for hh in range(2):
            s = s2[hh * ATT_BLOCK:(hh + 1) * ATT_BLOCK] + bias_ref[g, hh, table]
            m = jnp.max(s, axis=-1, keepdims=True)
            e = jnp.exp(s - m)
            ls.append(jnp.sum(e, axis=-1, keepdims=True))
            ms.append(m)
            es.append(e.astype(BF16))
        o2 = jnp.dot(jnp.concatenate(es, axis=0), v, preferred_element_type=F32)
        stats = (jnp.where(head0, o2[:ATT_BLOCK], o2[ATT_BLOCK:]),
                 jnp.where(head0, ms[0], ms[1]),
                 jnp.where(head0, ls[0], ls[1]))
        for ref, val in zip((so_ref, sm_ref, sl_ref), stats):
            if ATT_GROUPS[g][1] > 1:
                ref[g, pl.ds(q_start, ATT_BLOCK), :] = val
                continue
            for u in range(ATT_BLOCK // SUBLANES):
                start = (u % 2) * SUBLANES * dense_pitch + idx * (ATT_BLOCK // MAX_DIL) + u // 2
                ref[g, pl.ds(start, SUBLANES, stride=dense_pitch), :] = (
                    val[u * SUBLANES:(u + 1) * SUBLANES])

    for g, (window, dil) in enumerate(ATT_GROUPS):
        blocks_per_class = n_blocks // dil

        def body(it, carry, g=g, blocks_per_class=blocks_per_class):
            for u in range(ATT_UNROLL):
                block(g, it * ATT_UNROLL + u, blocks_per_class)
            return carry

        lax.fori_loop(0, n_blocks // ATT_UNROLL, body, 0)

    rows = seq // MAX_DIL
    for c16 in range(MAX_DIL):
        idxs = []
        for g, (window, dil) in enumerate(ATT_GROUPS):
            if dil == 1:
                idxs.append(pl.ds(c16 * dense_pitch, rows))
            elif dil == MAX_DIL:
                idxs.append(pl.ds(c16 * rows, rows))
            else:
                idxs.append(pl.ds((c16 % dil) * (seq // dil) + c16 // dil, rows,
                                  stride=MAX_DIL // dil))
        ms_ = [sm_ref[g, idx, :] for g, idx in enumerate(idxs)]
        m = jnp.maximum(jnp.maximum(ms_[0], ms_[1]), ms_[2])
        ws = [jnp.exp(mg - m) for mg in ms_]
        num = sum(w * so_ref[g, idx, :] for g, (w, idx) in enumerate(zip(ws, idxs)))
        den = sum(w * sl_ref[g, idx, :] for g, (w, idx) in enumerate(zip(ws, idxs)))
        out_ref[pl.ds(c16, rows, stride=MAX_DIL), :] = num / den


def _attention(qkv_groups, bias, batch, seq):
    in_specs, args = [], []
    for g, (window, dil) in enumerate(ATT_GROUPS):
        arr = qkv_groups[g].reshape(3 * N_PAIRS, batch, seq, HEAD_PAIR)
        for part in range(3):
            in_specs.append(pl.BlockSpec(
                (None, None, seq, HEAD_PAIR),
                lambda b, p, part=part: (part * N_PAIRS + p, b, 0, 0)))
            args.append(arr)
    in_specs.append(pl.BlockSpec(
        (N_GROUPS, 2, 2, ATT_BLOCK, 2 * ATT_BLOCK), lambda b, p: (0, p, 0, 0, 0)))
    args.append(bias)
    scratch = pltpu.VMEM((N_GROUPS, seq + MAX_DIL * SUBLANES, HEAD_PAIR), F32)
    out = pl.pallas_call(
        _attn_kernel,
        out_shape=jax.ShapeDtypeStruct((N_PAIRS, batch, seq, HEAD_PAIR), F32),
        grid=(batch, N_PAIRS),
        in_specs=in_specs,
        out_specs=pl.BlockSpec((None, None, seq, HEAD_PAIR), lambda b, p: (p, b, 0, 0)),
        scratch_shapes=[scratch, scratch, scratch],
        compiler_params=pltpu.CompilerParams(
            dimension_semantics=("arbitrary", "arbitrary"), vmem_limit_bytes=VMEM_LIMIT),
        name="dilated_attention",
    )(*args)
    return out.reshape(N_PAIRS, batch * seq, HEAD_PAIR)


def _attention_bias():
    n = ATT_BLOCK
    heads = jnp.arange(1, N_QKV_HEADS + 1, dtype=F32)
    slopes = jnp.exp2(-ALIBI_MAX_BIAS * heads / N_QKV_HEADS).reshape(N_GROUPS, HEADS_PER_GROUP)
    qi = jnp.arange(n)[:, None]
    ki = jnp.arange(2 * n)[None, :]
    tables = []
    for g, (window, dil) in enumerate(ATT_GROUPS):
        steps = n + qi - ki
        valid = (steps >= 0) & (steps <= n)
        steps_f = qi - ki
        valid_f = steps_f >= 0
        per_head = []
        for hh in range(HEADS_PER_GROUP):
            sl = slopes[g, hh]
            reg = jnp.where(valid, -sl * (dil * steps).astype(F32), NEG_INF)
            fst = jnp.where(valid_f, -sl * (dil * steps_f).astype(F32), NEG_INF)
            per_head.append(jnp.stack([reg, fst]))
        tables.append(jnp.stack(per_head))
    return jnp.stack(tables).astype(F32)


def _block_diag(w):
    depth = w.shape[0]
    per = LRU_GATE_BLK // LRU_BLOCK
    w = w.reshape(depth, D_LRU // LRU_GATE_BLK, per, LRU_BLOCK, LRU_BLOCK)
    eye = jnp.eye(per, dtype=w.dtype)
    full = w[:, :, :, :, None, :] * eye[None, None, :, None, :, None]
    return full.reshape(depth, D_LRU // LRU_GATE_BLK, LRU_GATE_BLK, LRU_GATE_BLK)


def kernel(x, ffn1_norm, ffn1_w_gate, ffn1_w_up, ffn1_w_down, mix_norm, w_in, q_norm, k_norm, conv_w, conv_b, gate_a_w, gate_a_b, gate_x_w, gate_x_b, lru_lambda, w_out, ffn2_norm, ffn2_w_gate, ffn2_w_up, ffn2_w_down):
    batch, seq, _ = x.shape
    depth = w_in.shape[0]
    n = batch * seq

    w_in_b = w_in.astype(BF16)
    w_gates = (0.5 * jnp.concatenate([_block_diag(gate_a_w), _block_diag(gate_x_w)],
                                     axis=-1)).astype(BF16)
    gate_a_b = 0.5 * gate_a_b
    gate_x_b = 0.5 * gate_x_b
    head_id = jnp.arange(MXU_DIM) // HEAD_DIM
    head_mean = ((head_id[:, None] == head_id[None, :]).astype(F32) / HEAD_DIM).astype(BF16)
    score_scale = 1.0 / math.sqrt(HEAD_DIM)
    q_gain = (jnp.tile(q_norm, (1, N_QKV_HEADS)) * score_scale).reshape(depth, 1, D_ATT_QKV)
    k_gain = jnp.tile(k_norm, (1, N_QKV_HEADS)).reshape(depth, 1, D_ATT_QKV)
    bias = _attention_bias()

    def rows(v):
        return v.reshape(depth, 1, v.shape[-1])

    mixer_params = (rows(mix_norm), w_in_b, head_mean, q_gain, k_gain, conv_w, rows(conv_b),
                    w_gates, rows(gate_a_b), rows(gate_x_b), rows(lru_lambda))
    xf = x.reshape(n, D_MODEL)
    for l in range(depth):
        xf = _ffn(xf, l, ffn1_norm, ffn1_w_gate, ffn1_w_up, ffn1_w_down)
        qkv0, qkv1, qkv2, y = _mixer_in(xf, l, mixer_params, batch, seq)
        att = _attention((qkv0, qkv1, qkv2), bias, batch, seq)
        xf = _ffn(xf, l, ffn2_norm, ffn2_w_gate, ffn2_w_up, ffn2_w_down,
                  mixer_out=(att, y, w_out))
    return xf.reshape(batch, seq, D_MODEL)
```
